```python
import jax, jax.numpy as jnp
from jax import lax
import numpy as np

D_MODEL = 1024
BATCH = 8
SEQ = 4096
DEPTH = 2

CTX_LEN = 256
GRID_W = 64

HG_HEADS = 4
HG_DK = 64
HG_DV = 64
HG_W = HG_HEADS * HG_DV
MLA_HEADS = 4
MLA_NOPE = 128
MLA_ROPE = 64
MLA_DV = 128
MLA_Q_RANK = 256
MLA_KV_RANK = 128
MLA_QK = MLA_NOPE + MLA_ROPE
MLA_W = MLA_HEADS * MLA_DV
MLA_SCALE = MLA_QK ** -0.5
GLA_HEADS = 4
GLA_DK = 32
GLA_DV = 64
GLA_GATE_RANK = 16
GLA_TAU = 16.0
GLA_W = GLA_HEADS * GLA_DV

MIX_W = HG_W + MLA_W + GLA_W
CHUNK = 64
Q_BLOCK = 128
ROPE_BASE = 10000.0
ROPE_FREQS = MLA_ROPE // 4
NORM_EPS = 1e-6
FORGET_MIN = 1e-6
DEEPNORM_ALPHA = (2 * DEPTH) ** 0.25
DEEPNORM_BETA = (8 * DEPTH) ** -0.25

IN_SPLITS = (HG_HEADS * HG_DK, HG_HEADS * HG_DK, HG_HEADS * HG_DK, HG_W, HG_W,
             MLA_Q_RANK, MLA_KV_RANK, MLA_ROPE, MLA_W,
             GLA_HEADS * GLA_DK, GLA_HEADS * GLA_DK, GLA_W, GLA_GATE_RANK, GLA_GATE_RANK, GLA_W)
IN_W = sum(IN_SPLITS)

kernel_name = "hymba_hgrn2_mla_gla_prefix_trunk"

F32 = jnp.float32


def split_cols(p):
    idx = [int(i) for i in np.cumsum(IN_SPLITS)[:-1]]
    return jnp.split(p, idx, axis=-1)


def layer_norm(x):
    x32 = x.astype(F32)
    mu = jnp.mean(x32, axis=-1, keepdims=True)
    var = jnp.mean(jnp.square(x32 - mu), axis=-1, keepdims=True)
    return ((x32 - mu) * lax.rsqrt(var + NORM_EPS)).astype(x.dtype)


def rms_norm(x, g):
    x32 = x.astype(F32)
    y = x32 * lax.rsqrt(jnp.mean(x32 * x32, axis=-1, keepdims=True) + NORM_EPS)
    return y.astype(x.dtype) * g


def axial_rope_tables(n):
    rows = n // GRID_W
    pos_r = jnp.repeat(jnp.arange(rows, dtype=F32), GRID_W)
    pos_c = jnp.tile(jnp.arange(GRID_W, dtype=F32), rows)
    inv = 1.0 / (ROPE_BASE ** (jnp.arange(ROPE_FREQS, dtype=F32) / ROPE_FREQS))
    ang = jnp.stack([pos_r, pos_c], axis=-1)[:, :, None] * inv
    return jnp.cos(ang), jnp.sin(ang)


def apply_axial_rope(x, cos, sin):
    xa = x.reshape(x.shape[:-1] + (2, 2, ROPE_FREQS))
    x1, x2 = xa[..., 0, :], xa[..., 1, :]
    cos = cos.astype(x.dtype)
    sin = sin.astype(x.dtype)
    out = jnp.stack([x1 * cos - x2 * sin, x2 * cos + x1 * sin], axis=-2)
    return out.reshape(x.shape)


def gated_recurrence(q, k, v, log_g, s0):
    B, L, H, _ = q.shape
    dv = v.shape[-1]
    nc = L // CHUNK

    def chunks(a):
        return a.astype(F32).reshape(B, nc, CHUNK, H, a.shape[-1]).transpose(1, 0, 3, 2, 4)

    lower = jnp.tril(jnp.ones((CHUNK, CHUNK), dtype=bool))[:, :, None]

    def step(s, inp):
        qc, kc, vc, gc = inp
        b = jnp.cumsum(gc, axis=2)
        diff = b[:, :, :, None, :] - b[:, :, None, :, :]
        decay = jnp.where(lower, jnp.exp(jnp.where(lower, diff, 0.0)), 0.0)
        scores = jnp.einsum('bhid,bhjd,bhijd->bhij', qc, kc, decay)
        o = (jnp.einsum('bhij,bhjv->bhiv', scores, vc)
             + jnp.einsum('bhid,bhdv->bhiv', qc * jnp.exp(b), s))
        b_end = b[:, :, -1:, :]
        s = (s * jnp.exp(b_end[:, :, 0, :])[..., None]
             + jnp.einsum('bhjd,bhjv->bhdv', kc * jnp.exp(b_end - b), vc))
        return s, o

    s_fin, o = lax.scan(step, s0.astype(F32), tuple(chunks(a) for a in (q, k, v, log_g)))
    o = o.transpose(1, 0, 3, 2, 4).reshape(B, L, H, dv)
    return o.astype(v.dtype), s_fin


def bidirectional_recurrence(lat, ctx, need_ctx_out):
    q0 = lat[0]
    B, H, dk = q0.shape[0], q0.shape[2], q0.shape[3]
    dv = lat[3].shape[3]
    s0 = jnp.zeros((B, H, dk, dv), F32)

    def scan_dir(inp, direction, s_init):
        q, k_f, k_b, v, g_f, g_b = inp
        k, g = (k_f, g_f) if direction == 0 else (k_b, g_b)
        if direction == 1:
            q, k, v, g = (jnp.flip(a, axis=1) for a in (q, k, v, g))
        o, s = gated_recurrence(q, k, v, g, s_init)
        if direction == 1:
            o = jnp.flip(o, axis=1)
        return o, s

    o_cf, s_cf = scan_dir(ctx, 0, s0)
    o_cb, s_cb = scan_dir(ctx, 1, s0)
    o_lf, _ = scan_dir(lat, 0, s_cf)
    o_lb, _ = scan_dir(lat, 1, s_cb)
    o_ctx = (o_cf + o_cb) if need_ctx_out else None
    return o_lf + o_lb, o_ctx


def hgrn2_inputs(zq, zf_fwd, zf_bwd, zi, lb):
    B, L, _ = zq.shape
    q = jax.nn.silu(zq).reshape(B, L, HG_HEADS, HG_DK)
    ks, gs = [], []
    for d, z in enumerate((zf_fwd, zf_bwd)):
        z32 = z.astype(F32)
        lbd = lb[d]
        f = lbd + (1.0 - lbd) * jax.nn.sigmoid(z32)
        log_f = jnp.log(jnp.maximum(f, FORGET_MIN))
        k = (1.0 - lbd) * jax.nn.sigmoid(-z32)
        ks.append(k.reshape(B, L, HG_HEADS, HG_DK))
        gs.append(log_f.reshape(B, L, HG_HEADS, HG_DK))
    v = zi.reshape(B, L, HG_HEADS, HG_DV)
    return (q, ks[0], ks[1], v, gs[0], gs[1])


def gla_inputs(zq, zk, zv, za_fwd, za_bwd, w_a2, b_a):
    B, L, _ = zq.shape
    q = zq.reshape(B, L, GLA_HEADS, GLA_DK) * (GLA_DK ** -0.5)
    k = zk.reshape(B, L, GLA_HEADS, GLA_DK)
    v = zv.reshape(B, L, GLA_HEADS, GLA_DV)
    gs = [(jax.nn.log_sigmoid((za @ w_a2[d] + b_a[d]).astype(F32)) / GLA_TAU)
          .reshape(B, L, GLA_HEADS, GLA_DK) for d, za in enumerate((za_fwd, za_bwd))]
    return (q, k, k, v, gs[0], gs[1])


def gated_head_norm(o, g_norm, gate):
    B, L, H, dv = o.shape
    return rms_norm(o, g_norm).reshape(B, L, H * dv) * jax.nn.silu(gate)


def mla_qkv(zcq, zckv, zkr, q_norm_g, kv_norm_g, w_uq, w_ukv, rope):
    B, L, _ = zcq.shape
    q = (rms_norm(zcq, q_norm_g) @ w_uq).reshape(B, L, MLA_HEADS, MLA_QK)
    kv = (rms_norm(zckv, kv_norm_g) @ w_ukv).reshape(B, L, MLA_HEADS, MLA_NOPE + MLA_DV)
    q_nope, q_rope = q[..., :MLA_NOPE], q[..., MLA_NOPE:]
    k_nope, v = kv[..., :MLA_NOPE], kv[..., MLA_NOPE:]
    k_rope = zkr
    if rope is not None:
        cos, sin = rope
        q_rope = apply_axial_rope(q_rope, cos[:, None], sin[:, None])
        k_rope = apply_axial_rope(k_rope, cos, sin)
    q = jnp.concatenate([q_nope, q_rope], axis=-1)
    k = jnp.concatenate(
        [k_nope, jnp.broadcast_to(k_rope[:, :, None, :], (B, L, MLA_HEADS, MLA_ROPE))], axis=-1)
    return q, k, v


def softmax_attend(q, k, v):
    s = jnp.einsum('bqhd,bkhd->bhqk', q, k).astype(F32) * MLA_SCALE
    p = jax.nn.softmax(s, axis=-1).astype(v.dtype)
    return jnp.einsum('bhqk,bkhd->bqhd', p, v)


def blockwise_attend(q, k, v):
    B, N, H, dq = q.shape
    nb = N // Q_BLOCK
    qb = q.reshape(B, nb, Q_BLOCK, H, dq).transpose(1, 0, 2, 3, 4)
    o = lax.map(lambda qblk: softmax_attend(qblk, k, v), qb)
    return o.transpose(1, 0, 2, 3, 4).reshape(B, N, H, v.shape[-1])


def trunk_layer(x, xc, mod_lat, mod_ctx, w_in, w_out, ln_g, ln_b, hg_lb, hg_norm_g,
                mla_q_norm_g, mla_kv_norm_g, mla_w_uq, mla_w_ukv, gla_w_a2, gla_b_a,
                gla_norm_g, rope, need_ctx_out):
    B, N, _ = x.shape
    Lc = xc.shape[1]
    shift, scale, gate = jnp.split(mod_lat[:, None, :], 3, axis=-1)
    shift_c, scale_c, gate_c = jnp.split(mod_ctx, 3, axis=-1)
    h = layer_norm(x) * (1.0 + scale) + shift
    hc = layer_norm(xc) * (1.0 + scale_c) + shift_c

    (hq, hff, hfb, hi, hgate, mcq, mckv, mkr, mgate,
     gq, gk, gv, gaf, gab, ggate) = split_cols(h @ w_in)
    (hq_c, hff_c, hfb_c, hi_c, hgate_c, mcq_c, mckv_c, mkr_c, mgate_c,
     gq_c, gk_c, gv_c, gaf_c, gab_c, ggate_c) = split_cols(hc @ w_in)

    hg_lat, hg_ctx = bidirectional_recurrence(
        hgrn2_inputs(hq, hff, hfb, hi, hg_lb),
        hgrn2_inputs(hq_c, hff_c, hfb_c, hi_c, hg_lb), need_ctx_out)
    gla_lat, gla_ctx = bidirectional_recurrence(
        gla_inputs(gq, gk, gv, gaf, gab, gla_w_a2, gla_b_a),
        gla_inputs(gq_c, gk_c, gv_c, gaf_c, gab_c, gla_w_a2, gla_b_a), need_ctx_out)
    q, k, v = mla_qkv(mcq, mckv, mkr, mla_q_norm_g, mla_kv_norm_g, mla_w_uq, mla_w_ukv, rope)
    q_c, k_c, v_c = mla_qkv(mcq_c, mckv_c, mkr_c, mla_q_norm_g, mla_kv_norm_g,
                            mla_w_uq, mla_w_ukv, None)
    mla_lat = blockwise_attend(q, jnp.concatenate([k_c, k], axis=1),
                               jnp.concatenate([v_c, v], axis=1))

    y = jnp.concatenate([gated_head_norm(hg_lat, hg_norm_g, hgate),
                         mla_lat.reshape(B, N, MLA_W) * jax.nn.silu(mgate),
                         gated_head_norm(gla_lat, gla_norm_g, ggate)], axis=-1)
    x_new = layer_norm(DEEPNORM_ALPHA * x + gate * (y @ w_out)) * ln_g + ln_b
    if not need_ctx_out:
        return x_new, xc

    mla_ctx = softmax_attend(q_c, k_c, v_c)
    yc = jnp.concatenate([gated_head_norm(hg_ctx, hg_norm_g, hgate_c),
                          mla_ctx.reshape(B, Lc, MLA_W) * jax.nn.silu(mgate_c),
                          gated_head_norm(gla_ctx, gla_norm_g, ggate_c)], axis=-1)
    xc_new = layer_norm(DEEPNORM_ALPHA * xc + gate_c * (yc @ w_out)) * ln_g + ln_b
    return x_new, xc_new


def setup_inputs(seed: int = 0) -> dict:
    key = jax.random.key(seed)
    ks = jax.random.split(key, 19)

    def nrm(k, shape, s):
        return jax.random.normal(k, shape, F32) * s

    D = D_MODEL
    return {
        "x": nrm(ks[0], (BATCH, SEQ, D), 1.0),
        "c": nrm(ks[1], (BATCH, D), 1.0),
        "ctx": nrm(ks[2], (BATCH, CTX_LEN, D), 1.0),
        "c_ctx": nrm(ks[3], (D,), 1.0),
        "w_mod": nrm(ks[4], (DEPTH, D, 3 * D), 0.5 * D ** -0.5),
        "b_mod": nrm(ks[5], (DEPTH, 3 * D), 0.02),
        "w_in": nrm(ks[6], (DEPTH, D, IN_W), D ** -0.5),
        "w_out": nrm(ks[7], (DEPTH, MIX_W, D), DEEPNORM_BETA * MIX_W ** -0.5),
        "ln_g": 1.0 + nrm(ks[8], (DEPTH, D), 0.02),
        "ln_b": nrm(ks[9], (DEPTH, D), 0.02),
        "hg_lb_logits": nrm(ks[10], (DEPTH, 2, HG_HEADS * HG_DK), 0.5),
        "hg_norm_g": 1.0 + nrm(ks[11], (DEPTH, HG_DV), 0.02),
        "mla_q_norm_g": 1.0 + nrm(ks[12], (DEPTH, MLA_Q_RANK), 0.02),
        "mla_kv_norm_g": 1.0 + nrm(ks[13], (DEPTH, MLA_KV_RANK), 0.02),
        "mla_w_uq": nrm(ks[14], (DEPTH, MLA_Q_RANK, MLA_HEADS * MLA_QK), MLA_Q_RANK ** -0.5),
        "mla_w_ukv": nrm(ks[15], (DEPTH, MLA_KV_RANK, MLA_HEADS * (MLA_NOPE + MLA_DV)),
                          MLA_KV_RANK ** -0.5),
        "gla_w_a2": nrm(ks[16], (DEPTH, 2, GLA_GATE_RANK, GLA_HEADS * GLA_DK),
                         GLA_GATE_RANK ** -0.5),
        "gla_b_a": nrm(ks[17], (DEPTH, 2, GLA_HEADS * GLA_DK), 0.1),
        "gla_norm_g": 1.0 + nrm(ks[18], (DEPTH, GLA_DV), 0.02),
    }


def reference(x, c, ctx, c_ctx, w_mod, b_mod, w_in, w_out, ln_g, ln_b, hg_lb_logits,
              hg_norm_g, mla_q_norm_g, mla_kv_norm_g, mla_w_uq, mla_w_ukv, gla_w_a2,
              gla_b_a, gla_norm_g):
    rope = axial_rope_tables(x.shape[1])
    lb_soft = jax.nn.softmax(hg_lb_logits.astype(F32), axis=0)
    hg_lb = jnp.clip(jnp.cumsum(lb_soft, axis=0) - lb_soft[0:1], 0.0, 1.0)
    silu_c = jax.nn.silu(c)
    silu_cc = jax.nn.silu(c_ctx)
    xc = ctx
    for l in range(DEPTH):
        mod_lat = silu_c @ w_mod[l] + b_mod[l]
        mod_ctx = silu_cc @ w_mod[l] + b_mod[l]
        x, xc = trunk_layer(x, xc, mod_lat, mod_ctx, w_in[l], w_out[l], ln_g[l], ln_b[l],
                            hg_lb[l].astype(x.dtype), hg_norm_g[l], mla_q_norm_g[l],
                            mla_kv_norm_g[l], mla_w_uq[l], mla_w_ukv[l], gla_w_a2[l],
                            gla_b_a[l], gla_norm_g[l], rope, l < DEPTH - 1)
    return x
```

```python
import functools

import numpy as np
import jax
import jax.numpy as jnp
from jax import lax
from jax.experimental import pallas as pl
from jax.experimental.pallas import tpu as pltpu

F32 = jnp.float32
BF16 = jnp.bfloat16
HIGHEST = lax.Precision.HIGHEST

D_MODEL = 1024
DEPTH = 2
GRID_W = 64
HEADS = 4
HG_DK = 64
GLA_DK = 32
REC_DV = 64
REC_W = HEADS * REC_DV
MLA_NOPE = 128
MLA_ROPE = 64
MLA_DV = 128
MLA_Q_RANK = 256
MLA_KV_RANK = 128
MLA_QK = MLA_NOPE + MLA_ROPE
MLA_W = HEADS * MLA_DV
MLA_SCALE = MLA_QK ** -0.5
GLA_GATE_RANK = 16
GLA_TAU = 16.0
ROPE_BASE = 10000.0
ROPE_FREQS = MLA_ROPE // 4
NORM_EPS = 1e-6
FORGET_MIN = 1e-6
DEEPNORM_ALPHA = (2 * DEPTH) ** 0.25

CHUNK = 64
SUB = 16
NSUB = CHUNK // SUB
LANE = 128

HG_COLS = 5 * REC_W
MLA_COLS = 1024
GLA_COLS = 896
IN_COLS = HG_COLS + MLA_COLS + GLA_COLS

VMEM_LIMIT = 56 * 1024 * 1024


def _cparams(sem):
    return pltpu.CompilerParams(dimension_semantics=sem, vmem_limit_bytes=VMEM_LIMIT)


def _dot(a, b):
    return jnp.dot(a, b, preferred_element_type=F32)


def _dot_nt(a, b):
    return lax.dot_general(a, b, (((1,), (1,)), ((), ())), preferred_element_type=F32)


def _dot_tn(a, b):
    return lax.dot_general(a, b, (((0,), (0,)), ((), ())), preferred_element_type=F32)


def _silu(x):
    return x * jax.nn.sigmoid(x)


def _iota_div(shape, axis, div):
    assert div & (div - 1) == 0
    return lax.shift_right_logical(lax.broadcasted_iota(jnp.int32, shape, axis),
                                   jnp.int32(div.bit_length() - 1))


def _mod_kernel(c_ref, w_ref, b_ref, o_ref):
    s = _silu(c_ref[...])
    o_ref[0] = jnp.dot(s, w_ref[0], precision=HIGHEST, preferred_element_type=F32) + b_ref[0]


def _modulation(c_rows, w_mod, b_mod):
    rows = c_rows.shape[0]
    nt = 3 * D_MODEL // 1024
    return pl.pallas_call(
        _mod_kernel,
        grid=(DEPTH, nt),
        in_specs=[pl.BlockSpec((rows, D_MODEL), lambda l, j: (0, 0)),
                  pl.BlockSpec((1, D_MODEL, 1024), lambda l, j: (l, 0, j)),
                  pl.BlockSpec((1, 1, 1024), lambda l, j: (l, 0, j))],
        out_specs=pl.BlockSpec((1, rows, 1024), lambda l, j: (l, 0, j)),
        out_shape=jax.ShapeDtypeStruct((DEPTH, rows, 3 * D_MODEL), F32),
        compiler_params=_cparams(("arbitrary", "arbitrary")),
        name="modulation",
    )(c_rows, w_mod, b_mod.reshape(DEPTH, 1, 3 * D_MODEL))


def _layer_norm(x):
    mu = jnp.mean(x, axis=-1, keepdims=True)
    xc = x - mu
    var = jnp.mean(xc * xc, axis=-1, keepdims=True)
    return xc * lax.rsqrt(var + NORM_EPS)


def _inproj_kernel(x_ref, shift_ref, scale_ref, w_ref, ohg_ref, omla_ref, ogla_ref):
    h = _layer_norm(x_ref[0]) * (1.0 + scale_ref[0]) + shift_ref[0]
    hb = h.astype(BF16)
    ohg_ref[0] = _dot(hb, w_ref[:, 0:HG_COLS])
    omla_ref[0] = _dot(hb, w_ref[:, HG_COLS:HG_COLS + MLA_COLS])
    ogla_ref[0] = _dot(hb, w_ref[:, HG_COLS + MLA_COLS:IN_COLS])


def _in_projection(x, mod, mod_row, w_in_packed, tm):
    B, T, _ = x.shape
    return pl.pallas_call(
        _inproj_kernel,
        grid=(B, T // tm),
        in_specs=[pl.BlockSpec((1, tm, D_MODEL), lambda b, i: (b, i, 0)),
                  pl.BlockSpec((1, 1, D_MODEL), lambda b, i: (mod_row(b), 0, 0)),
                  pl.BlockSpec((1, 1, D_MODEL), lambda b, i: (mod_row(b), 0, 1)),
                  pl.BlockSpec((D_MODEL, IN_COLS), lambda b, i: (0, 0))],
        out_specs=[pl.BlockSpec((1, tm, HG_COLS), lambda b, i: (b, i, 0)),
                   pl.BlockSpec((1, tm, MLA_COLS), lambda b, i: (b, i, 0)),
                   pl.BlockSpec((1, tm, GLA_COLS), lambda b, i: (b, i, 0))],
        out_shape=[jax.ShapeDtypeStruct((B, T, HG_COLS), F32),
                   jax.ShapeDtypeStruct((B, T, MLA_COLS), F32),
                   jax.ShapeDtypeStruct((B, T, GLA_COLS), F32)],
        compiler_params=_cparams(("arbitrary", "arbitrary")),
        name="in_projection",
    )(x, mod, mod, w_in_packed)


def _split3(g):
    hi = g.astype(BF16)
    r1 = g - hi.astype(F32)
    mid = r1.astype(BF16)
    lo = (r1 - mid.astype(F32)).astype(BF16)
    return hi, mid, lo


def _rec_chunk(q, k, v, g, st_ref, a_loc, head_ones, rev, dk):
    dkt = HEADS * dk
    ghi, gmid, glo = _split3(g)
    lcum = _dot(a_loc, ghi) + _dot(a_loc, gmid) + _dot(a_loc, glo)

    phys = [(NSUB - 1 - p) if rev else p for p in range(NSUB)]
    last_row = [(I * SUB) if rev else (I * SUB + SUB - 1) for I in phys]
    ltot = [lcum[r:r + 1, :] for r in last_row]
    rows = lambda I: slice(I * SUB, (I + 1) * SUB)

    q_in = [None] * NSUB
    k_out = [None] * NSUB
    for p, I in enumerate(phys):
        lc = lcum[rows(I), :]
        q_in[p] = q[rows(I), :] * jnp.exp(lc)
        k_out[p] = k[rows(I), :] * jnp.exp(ltot[p] - lc)

    pre = [None] * NSUB
    suf = [None] * NSUB
    acc = jnp.zeros_like(ltot[0])
    for p in range(NSUB):
        pre[p] = acc
        acc = acc + ltot[p]
    total = acc
    acc = jnp.zeros_like(ltot[0])
    for p in reversed(range(NSUB)):
        suf[p] = acc
        acc = acc + ltot[p]

    q_hat = [q_in[p] * jnp.exp(pre[p]) for p in range(NSUB)]
    k_hat = [k_out[p] * jnp.exp(suf[p]) for p in range(NSUB)]
    q_d2 = {p: q_in[p] * jnp.exp(ltot[p - 1]) for p in range(2, NSUB)}
    q_d3 = {p: q_in[p] * jnp.exp(ltot[p - 1] + ltot[p - 2]) for p in range(3, NSUB)}

    def phys_concat(parts):
        order = sorted(range(NSUB), key=lambda p: phys[p])
        return jnp.concatenate([parts[p] for p in order], axis=0)

    k_out_all = phys_concat(k_out).astype(BF16)
    rk = _iota_div((HEADS * CHUNK, dkt), 0, CHUNK)
    ck = _iota_div((HEADS * CHUNK, dkt), 1, dk)
    kbd_t = jnp.where(rk == ck, jnp.concatenate([k_out_all] * HEADS, axis=0), 0)
    lhs = jnp.concatenate([q_in[p] for p in range(1, NSUB)]
                          + [q_d2[p] for p in range(2, NSUB)]
                          + [q_d3[p] for p in range(3, NSUB)], axis=0).astype(BF16)
    sc = _dot_nt(lhs, kbd_t)
    col_blk = _iota_div((SUB, HEADS * CHUNK), 1, SUB) & (NSUB - 1)
    col_p = (NSUB - 1 - col_blk) if rev else col_blk
    s1 = {p: sc[(p - 1) * SUB:p * SUB] for p in range(1, NSUB)}
    s2 = {p: sc[(NSUB - 1 + p - 2) * SUB:(NSUB - 1 + p - 1) * SUB] for p in range(2, NSUB)}
    s3 = {p: sc[(2 * NSUB - 3 + p - 3) * SUB:(2 * NSUB - 3 + p - 2) * SUB] for p in range(3, NSUB)}
    p_off = [None] * NSUB
    p_off[0] = jnp.zeros((SUB, HEADS * CHUNK), F32)
    for p in range(1, NSUB):
        acc_p = jnp.where(col_p == p - 1, s1[p], 0.0)
        if p >= 2:
            acc_p = jnp.where(col_p == p - 2, s2[p], acc_p)
        if p >= 3:
            acc_p = jnp.where(col_p == p - 3, s3[p], acc_p)
        p_off[p] = acc_p
    p_off_all = phys_concat(p_off).astype(BF16)
    rv = _iota_div((HEADS * CHUNK, REC_W), 0, CHUNK)
    cv = _iota_div((HEADS * CHUNK, REC_W), 1, REC_DV)
    vb = v.astype(BF16)
    vbd = jnp.where(rv == cv, jnp.concatenate([vb] * HEADS, axis=0), 0)
    o = _dot(p_off_all, vbd)

    st = st_ref[...]
    q_hat_all = phys_concat(q_hat).astype(BF16)
    o = o + _dot_nt(q_hat_all, st.astype(BF16))
    k_hat_all = phys_concat(k_hat).astype(BF16)
    rs = _iota_div((REC_W, dkt), 0, REC_DV)
    cs = _iota_div((REC_W, dkt), 1, dk)
    upd = _dot_tn(vb, k_hat_all)
    st_ref[...] = st * jnp.exp(total) + jnp.where(rs == cs, upd, 0.0)

    ri = lax.broadcasted_iota(jnp.int32, (SUB, dkt), 0)
    pieces = []
    for I in range(NSUB):
        qi = q[rows(I), :]
        lc = lcum[rows(I), :]
        for jj in range(SUB):
            r = I * SUB + jj
            valid = (ri <= jj) if rev else (ri >= jj)
            e = jnp.exp(jnp.where(valid, lc - lcum[r:r + 1, :], -1e30))
            pieces.append((qi * k[r:r + 1, :] * e).astype(BF16))
    t_all = jnp.concatenate(pieces, axis=0)
    w_all = _dot(t_all, head_ones)
    o_diag = []
    for I in range(NSUB):
        acc_i = jnp.zeros((SUB, REC_W), F32)
        for jj in range(SUB):
            r = I * SUB + jj
            n = (I * SUB + jj) * SUB
            acc_i = acc_i + w_all[n:n + SUB, :] * v[r:r + 1, :]
        o_diag.append(acc_i)
    return o + jnp.concatenate(o_diag, axis=0)


def _hg_prep(z, lb_f, lb_b):
    q = _silu(z[:, 0:REC_W])
    v = z[:, 3 * REC_W:4 * REC_W]
    out = []
    for d, lb in enumerate((lb_f, lb_b)):
        zz = z[:, (1 + d) * REC_W:(2 + d) * REC_W]
        sg = jax.nn.sigmoid(zz)
        f = lb + (1.0 - lb) * sg
        g = jnp.log(jnp.maximum(f, FORGET_MIN))
        k = (1.0 - lb) * jax.nn.sigmoid(-zz)
        out.append((k, g))
    return q, out, v


def _log_sigmoid(x):
    return jnp.minimum(x, 0.0) - jnp.log(1.0 + jnp.exp(-jnp.abs(x)))


def _gla_prep(z, wg_ref, bg_ref):
    dkt = HEADS * GLA_DK
    q = z[:, 0:dkt] * (GLA_DK ** -0.5)
    k = z[:, dkt:2 * dkt]
    v = z[:, 2 * dkt:2 * dkt + REC_W]
    za = z[:, GLA_COLS - LANE:GLA_COLS].astype(BF16)
    out = []
    for d in range(2):
        x = _dot(za, wg_ref[d]) + bg_ref[d]
        out.append((k, _log_sigmoid(x) * (1.0 / GLA_TAU)))
    return q, out, v


def _rec_kernel(*refs, mixer, dk):
    if mixer == "hg":
        (zf_ref, zb_ref, lb_ref, aloc_ref, ones_ref, sf0_ref, sb0_ref,
         of_ref, ob_ref, sf1_ref, sb1_ref, stf, stb) = refs
    else:
        (zf_ref, zb_ref, wg_ref, bg_ref, aloc_ref, ones_ref, sf0_ref, sb0_ref,
         of_ref, ob_ref, sf1_ref, sb1_ref, stf, stb) = refs
    c = pl.program_id(1)

    @pl.when(c == 0)
    def _():
        stf[...] = sf0_ref[0]
        stb[...] = sb0_ref[0]

    head_ones = ones_ref[...]
    for d, (z_ref, o_ref, st) in enumerate(((zf_ref, of_ref, stf), (zb_ref, ob_ref, stb))):
        z = z_ref[0]
        if mixer == "hg":
            q, kg, v = _hg_prep(z, lb_ref[0:1, :], lb_ref[1:2, :])
        else:
            q, kg, v = _gla_prep(z, wg_ref, bg_ref)
        k, g = kg[d]
        o_ref[0] = _rec_chunk(q, k, v, g, st, aloc_ref[d], head_ones, rev=(d == 1), dk=dk)

    @pl.when(c == pl.num_programs(1) - 1)
    def _():
        sf1_ref[0] = stf[...]
        sb1_ref[0] = stb[...]


def _rec_constants(dk):
    a = np.zeros((2, CHUNK, CHUNK), np.float32)
    for i in range(CHUNK):
        for j in range(CHUNK):
            if i // SUB == j // SUB:
                a[0, i, j] = 1.0 if j <= i else 0.0
                a[1, i, j] = 1.0 if j >= i else 0.0
    dkt = HEADS * dk
    ones = np.zeros((dkt, REC_W), np.float32)
    for r in range(dkt):
        ones[r, (r // dk) * REC_DV:(r // dk + 1) * REC_DV] = 1.0
    return jnp.asarray(a, BF16), jnp.asarray(ones, BF16)


def _recurrence(z, mixer, params, s0f, s0b):
    B, T, cols = z.shape
    dk = HG_DK if mixer == "hg" else GLA_DK
    dkt = HEADS * dk
    nc = T // CHUNK
    a_loc, head_ones = _rec_constants(dk)
    full = lambda shape: pl.BlockSpec(shape, lambda b, c: (0,) * len(shape))
    in_specs = [pl.BlockSpec((1, CHUNK, cols), lambda b, c: (b, c, 0)),
                pl.BlockSpec((1, CHUNK, cols), lambda b, c: (b, nc - 1 - c, 0))]
    in_specs += [full(p.shape) for p in params]
    in_specs += [full(a_loc.shape), full(head_ones.shape),
                 pl.BlockSpec((1, REC_W, dkt), lambda b, c: (b, 0, 0)),
                 pl.BlockSpec((1, REC_W, dkt), lambda b, c: (b, 0, 0))]
    out_specs = [pl.BlockSpec((1, CHUNK, REC_W), lambda b, c: (b, c, 0)),
                 pl.BlockSpec((1, CHUNK, REC_W), lambda b, c: (b, nc - 1 - c, 0)),
                 pl.BlockSpec((1, REC_W, dkt), lambda b, c: (b, 0, 0)),
                 pl.BlockSpec((1, REC_W, dkt), lambda b, c: (b, 0, 0))]
    out_shape = [jax.ShapeDtypeStruct((B, T, REC_W), F32)] * 2 + [jax.ShapeDtypeStruct((B, REC_W, dkt), F32)] * 2
    return pl.pallas_call(
        functools.partial(_rec_kernel, mixer=mixer, dk=dk),
        grid=(B, nc),
        in_specs=in_specs,
        out_specs=out_specs,
        out_shape=out_shape,
        scratch_shapes=[pltpu.VMEM((REC_W, dkt), F32), pltpu.VMEM((REC_W, dkt), F32)],
        compiler_params=_cparams(("arbitrary", "arbitrary")),
        name="recurrence_" + mixer,
    )(z, z, *params, a_loc, head_ones, s0f, s0b)


def _rms_norm(x, g):
    return x * lax.rsqrt(jnp.mean(x * x, axis=-1, keepdims=True) + NORM_EPS) * g


def _swap_halves(x):
    n = x.shape[-1]
    lane = lax.broadcasted_iota(jnp.int32, x.shape, x.ndim - 1)
    first_half = (lane & (2 * ROPE_FREQS - 1)) < ROPE_FREQS
    return jnp.where(first_half, pltpu.roll(x, n - ROPE_FREQS, x.ndim - 1),
                     pltpu.roll(x, ROPE_FREQS, x.ndim - 1))


def _mla_up_kernel(z_ref, qg_ref, kvg_ref, wuq_ref, wukv_ref, cos_ref, sin_ref,
                   q_ref, k_ref, v_ref, *, rope):
    z = z_ref[0]
    cq = _rms_norm(z[:, 0:MLA_Q_RANK], qg_ref[...]).astype(BF16)
    ckv = _rms_norm(z[:, MLA_Q_RANK:MLA_Q_RANK + MLA_KV_RANK], kvg_ref[...]).astype(BF16)
    kr = z[:, MLA_Q_RANK + MLA_KV_RANK:MLA_Q_RANK + MLA_KV_RANK + LANE]
    qf = _dot(cq, wuq_ref[...])
    kvf = _dot(ckv, wukv_ref[...])
    q_rope = qf[:, HEADS * MLA_NOPE:]
    if rope:
        cos = cos_ref[...]
        sin = sin_ref[...]
        cos_q = jnp.concatenate([cos, cos], axis=-1)
        sin_q = jnp.concatenate([sin, sin], axis=-1)
        q_rope = q_rope * cos_q + _swap_halves(q_rope) * sin_q
        kr = kr * cos + _swap_halves(kr) * sin
    kr = kr[:, 0:MLA_ROPE]
    for h in range(HEADS):
        q_ref[0, h, :, 0:MLA_NOPE] = qf[:, h * MLA_NOPE:(h + 1) * MLA_NOPE].astype(BF16)
        q_ref[0, h, :, MLA_NOPE:MLA_QK] = q_rope[:, h * MLA_ROPE:(h + 1) * MLA_ROPE].astype(BF16)
        k_ref[0, h, :, 0:MLA_NOPE] = kvf[:, h * MLA_NOPE:(h + 1) * MLA_NOPE].astype(BF16)
        k_ref[0, h, :, MLA_NOPE:MLA_QK] = kr.astype(BF16)
        v_ref[0, h] = kvf[:, HEADS * MLA_NOPE + h * MLA_DV:HEADS * MLA_NOPE + (h + 1) * MLA_DV].astype(BF16)


def _mla_up(z_mla, q_norm_g, kv_norm_g, wuq_packed, wukv_packed, cos_t, sin_t, rope, tm):
    B, T, _ = z_mla.shape
    full = lambda shape: pl.BlockSpec(shape, lambda b, i: (0,) * len(shape))
    return pl.pallas_call(
        functools.partial(_mla_up_kernel, rope=rope),
        grid=(B, T // tm),
        in_specs=[pl.BlockSpec((1, tm, MLA_COLS), lambda b, i: (b, i, 0)),
                  full((1, MLA_Q_RANK)), full((1, MLA_KV_RANK)),
                  full(wuq_packed.shape), full(wukv_packed.shape),
                  pl.BlockSpec((tm, LANE), lambda b, i: (i, 0)),
                  pl.BlockSpec((tm, LANE), lambda b, i: (i, 0))],
        out_specs=[pl.BlockSpec((1, HEADS, tm, MLA_QK), lambda b, i: (b, 0, i, 0)),
                   pl.BlockSpec((1, HEADS, tm, MLA_QK), lambda b, i: (b, 0, i, 0)),
                   pl.BlockSpec((1, HEADS, tm, MLA_DV), lambda b, i: (b, 0, i, 0))],
        out_shape=[jax.ShapeDtypeStruct((B, HEADS, T, MLA_QK), BF16),
                   jax.ShapeDtypeStruct((B, HEADS, T, MLA_QK), BF16),
                   jax.ShapeDtypeStruct((B, HEADS, T, MLA_DV), BF16)],
        compiler_params=_cparams(("arbitrary", "arbitrary")),
        name="mla_up_rope" if rope else "mla_up",
    )(z_mla, q_norm_g.reshape(1, -1), kv_norm_g.reshape(1, -1), wuq_packed, wukv_packed, cos_t, sin_t)


def _attn_kernel(q_ref, k_ref, v_ref, o_ref):
    s = _dot_nt(q_ref[0, 0], k_ref[0, 0]) * MLA_SCALE
    m = jnp.max(s, axis=-1, keepdims=True)
    p = jnp.exp(s - m)
    l = jnp.sum(p, axis=-1, keepdims=True)
    o = _dot(p.astype(BF16), v_ref[0, 0])
    o_ref[0] = o / l


def _attention(q, k, v, tq):
    B, H, N, _ = q.shape
    M = k.shape[2]
    return pl.pallas_call(
        _attn_kernel,
        grid=(B, H, N // tq),
        in_specs=[pl.BlockSpec((1, 1, tq, MLA_QK), lambda b, h, i: (b, h, i, 0)),
                  pl.BlockSpec((1, 1, M, MLA_QK), lambda b, h, i: (b, h, 0, 0)),
                  pl.BlockSpec((1, 1, M, MLA_DV), lambda b, h, i: (b, h, 0, 0))],
        out_specs=pl.BlockSpec((1, tq, MLA_DV), lambda b, h, i: (b, i, h)),
        out_shape=jax.ShapeDtypeStruct((B, N, H * MLA_DV), F32),
        compiler_params=_cparams(("arbitrary", "arbitrary", "arbitrary")),
        name="attention",
    )(q, k, v)


def _head_rms(o, ones_bf, g):
    sq = o * o
    hi = sq.astype(BF16)
    lo = (sq - hi.astype(F32)).astype(BF16)
    ms = (_dot(hi, ones_bf) + _dot(lo, ones_bf)) * (1.0 / REC_DV)
    return o * lax.rsqrt(ms + NORM_EPS) * g


def _out_kernel(x_ref, gate_ref, zhg_ref, zmla_ref, zgla_ref, hgf_ref, hgb_ref, att_ref, glf_ref, glb_ref,
                ones_ref, hgn_ref, gln_ref, w_ref, lng_ref, lnb_ref, o_ref):
    ones_bf = ones_ref[...]
    y_hg = _head_rms(hgf_ref[0] + hgb_ref[0], ones_bf, hgn_ref[...]) * _silu(zhg_ref[0])
    y_mla = att_ref[0] * _silu(zmla_ref[0])
    y_gla = _head_rms(glf_ref[0] + glb_ref[0], ones_bf, gln_ref[...]) * _silu(zgla_ref[0])
    proj = (_dot(y_hg.astype(BF16), w_ref[0:REC_W, :])
            + _dot(y_mla.astype(BF16), w_ref[REC_W:REC_W + MLA_W, :])
            + _dot(y_gla.astype(BF16), w_ref[REC_W + MLA_W:, :]))
    r = DEEPNORM_ALPHA * x_ref[0] + gate_ref[0] * proj
    o_ref[0] = _layer_norm(r) * lng_ref[...] + lnb_ref[...]


def _output_block(x, mod, mod_row, z_hg, z_mla, z_gla, hg_f, hg_b, att, gl_f, gl_b,
                  hg_norm_g, gla_norm_g, w_out_bf, ln_g, ln_b, tm):
    B, T, _ = x.shape
    ones = np.zeros((REC_W, REC_W), np.float32)
    for r in range(REC_W):
        ones[r, (r // REC_DV) * REC_DV:(r // REC_DV + 1) * REC_DV] = 1.0
    ones = jnp.asarray(ones, BF16)
    row = lambda w: pl.BlockSpec((1, tm, w), lambda b, i: (b, i, 0))
    full = lambda shape: pl.BlockSpec(shape, lambda b, i: (0,) * len(shape))
    return pl.pallas_call(
        _out_kernel,
        grid=(B, T // tm),
        in_specs=[row(D_MODEL),
                  pl.BlockSpec((1, 1, D_MODEL), lambda b, i: (mod_row(b), 0, 2)),
                  pl.BlockSpec((1, tm, REC_W), lambda b, i: (b, i, HG_COLS // REC_W - 1)),
                  pl.BlockSpec((1, tm, MLA_W), lambda b, i: (b, i, 1)),
                  pl.BlockSpec((1, tm, REC_W), lambda b, i: (b, i, 2)),
                  row(REC_W), row(REC_W), row(MLA_W), row(REC_W), row(REC_W),
                  full((REC_W, REC_W)), full((1, REC_W)), full((1, REC_W)),
                  full((D_MODEL, D_MODEL)), full((1, D_MODEL)), full((1, D_MODEL))],
        out_specs=row(D_MODEL),
        out_shape=jax.ShapeDtypeStruct((B, T, D_MODEL), F32),
        compiler_params=_cparams(("arbitrary", "arbitrary")),
        name="output_block",
    )(x, mod, z_hg, z_mla, z_gla, hg_f, hg_b, att, gl_f, gl_b, ones,
      jnp.tile(hg_norm_g, HEADS).reshape(1, REC_W), jnp.tile(gla_norm_g, HEADS).reshape(1, REC_W),
      w_out_bf, ln_g.reshape(1, D_MODEL), ln_b.reshape(1, D_MODEL))


def _pack_w_in(w):
    o = np.cumsum([0, 256, 256, 256, 256, 256, 256, 128, 64, 512, 128, 128, 256, 16, 16, 256])
    col = lambda i: w[:, int(o[i]):int(o[i + 1])]
    zeros = lambda n: jnp.zeros((w.shape[0], n), w.dtype)
    hg = [col(0), col(1), col(2), col(3), col(4)]
    mla = [col(5), col(6), col(7), zeros(64), col(8)]
    gla = [col(9), col(10), col(11), col(14), col(12), col(13), zeros(96)]
    return jnp.concatenate(hg + mla + gla, axis=1).astype(BF16)


def _pack_wuq(w):
    w4 = w.reshape(MLA_Q_RANK, HEADS, MLA_QK)
    return jnp.concatenate([w4[:, :, :MLA_NOPE].reshape(MLA_Q_RANK, -1),
                            w4[:, :, MLA_NOPE:].reshape(MLA_Q_RANK, -1)], axis=1).astype(BF16)


def _pack_wukv(w):
    w4 = w.reshape(MLA_KV_RANK, HEADS, MLA_NOPE + MLA_DV)
    return jnp.concatenate([w4[:, :, :MLA_NOPE].reshape(MLA_KV_RANK, -1),
                            w4[:, :, MLA_NOPE:].reshape(MLA_KV_RANK, -1)], axis=1).astype(BF16)


def _pack_gla_gate(w_a2, b_a):
    dkt = HEADS * GLA_DK
    wg = jnp.zeros((2, LANE, dkt), F32)
    wg = wg.at[0, 0:GLA_GATE_RANK].set(w_a2[0])
    wg = wg.at[1, GLA_GATE_RANK:2 * GLA_GATE_RANK].set(w_a2[1])
    return wg.astype(BF16), b_a.reshape(2, 1, dkt)


def _rope_tables(n):
    rows = n // GRID_W
    pos_r = jnp.repeat(jnp.arange(rows, dtype=F32), GRID_W)
    pos_c = jnp.tile(jnp.arange(GRID_W, dtype=F32), rows)
    inv = 1.0 / (ROPE_BASE ** (jnp.arange(ROPE_FREQS, dtype=F32) / ROPE_FREQS))
    ang = jnp.stack([pos_r, pos_c], axis=-1)[:, :, None] * inv
    cos = jnp.cos(ang)
    sin = jnp.sin(ang)
    cos64 = jnp.concatenate([cos, cos], axis=-1).reshape(n, MLA_ROPE)
    sin64 = jnp.concatenate([-sin, sin], axis=-1).reshape(n, MLA_ROPE)
    return jnp.tile(cos64, (1, LANE // MLA_ROPE)), jnp.tile(sin64, (1, LANE // MLA_ROPE))


def _pick_tile(t, pref):
    while t % pref:
        pref //= 2
    return pref


def kernel(x, c, ctx, c_ctx, w_mod, b_mod, w_in, w_out, ln_g, ln_b, hg_lb_logits, hg_norm_g,
           mla_q_norm_g, mla_kv_norm_g, mla_w_uq, mla_w_ukv, gla_w_a2, gla_b_a, gla_norm_g):
    B, N, _ = x.shape
    Lc = ctx.shape[1]
    cos_t, sin_t = _rope_tables(N)
    cos_c = jnp.zeros((Lc, LANE), F32)

    lb_soft = jax.nn.softmax(hg_lb_logits.astype(F32), axis=0)
    hg_lb = jnp.clip(jnp.cumsum(lb_soft, axis=0) - lb_soft[0:1], 0.0, 1.0)

    n_rows = -(-(B + 1) // 8) * 8
    c_rows = jnp.zeros((n_rows, D_MODEL), F32).at[0:B].set(c).at[B].set(c_ctx)
    mods = _modulation(c_rows, w_mod, b_mod)

    lat_row = lambda b: b
    ctx_row = lambda b: B
    tm_l, tm_c = _pick_tile(N, 256), _pick_tile(Lc, 256)
    zeros_state = lambda dkt: jnp.zeros((B, REC_W, dkt), F32)

    xc = ctx
    for l in range(DEPTH):
        need_ctx = l < DEPTH - 1
        mod = mods[l].reshape(n_rows, 1, 3 * D_MODEL)
        w_in_p = _pack_w_in(w_in[l])
        zl_hg, zl_mla, zl_gla = _in_projection(x, mod, lat_row, w_in_p, tm_l)
        zc_hg, zc_mla, zc_gla = _in_projection(xc, mod, ctx_row, w_in_p, tm_c)

        hg_params = (hg_lb[l],)
        hc_f, hc_b, s_f, s_b = _recurrence(zc_hg, "hg", hg_params, zeros_state(HEADS * HG_DK),
                                           zeros_state(HEADS * HG_DK))
        hl_f, hl_b, _, _ = _recurrence(zl_hg, "hg", hg_params, s_f, s_b)
        gla_params = _pack_gla_gate(gla_w_a2[l], gla_b_a[l])
        gc_f, gc_b, s_f, s_b = _recurrence(zc_gla, "gla", gla_params, zeros_state(HEADS * GLA_DK),
                                           zeros_state(HEADS * GLA_DK))
        gl_f, gl_b, _, _ = _recurrence(zl_gla, "gla", gla_params, s_f, s_b)

        wuq_p, wukv_p = _pack_wuq(mla_w_uq[l]), _pack_wukv(mla_w_ukv[l])
        q_l, k_l, v_l = _mla_up(zl_mla, mla_q_norm_g[l], mla_kv_norm_g[l], wuq_p, wukv_p,
                                cos_t, sin_t, True, tm_l)
        q_c, k_c, v_c = _mla_up(zc_mla, mla_q_norm_g[l], mla_kv_norm_g[l], wuq_p, wukv_p,
                                cos_c, cos_c, False, tm_c)
        att_l = _attention(q_l, jnp.concatenate([k_c, k_l], axis=2),
                           jnp.concatenate([v_c, v_l], axis=2), _pick_tile(N, 256))

        w_out_bf = w_out[l].astype(BF16)
        x_new = _output_block(x, mod, lat_row, zl_hg, zl_mla, zl_gla, hl_f, hl_b, att_l, gl_f, gl_b,
                              hg_norm_g[l], gla_norm_g[l], w_out_bf, ln_g[l], ln_b[l], tm_l)
        if need_ctx:
            att_c = _attention(q_c, k_c, v_c, _pick_tile(Lc, 256))
            xc = _output_block(xc, mod, ctx_row, zc_hg, zc_mla, zc_gla, hc_f, hc_b, att_c, gc_f, gc_b,
                               hg_norm_g[l], gla_norm_g[l], w_out_bf, ln_g[l], ln_b[l], tm_c)
        x = x_new
    return x
```

```python
import functools

import numpy as np
import jax
import jax.numpy as jnp
from jax import lax
from jax.experimental import pallas as pl
from jax.experimental.pallas import tpu as pltpu

F32 = jnp.float32
BF16 = jnp.bfloat16
HIGHEST = lax.Precision.HIGHEST

D_MODEL = 1024
DEPTH = 2
GRID_W = 64
HEADS = 4
HG_DK = 64
GLA_DK = 32
REC_DV = 64
REC_W = HEADS * REC_DV
MLA_NOPE = 128
MLA_ROPE = 64
MLA_DV = 128
MLA_Q_RANK = 256
MLA_KV_RANK = 128
MLA_QK = MLA_NOPE + MLA_ROPE
MLA_W = HEADS * MLA_DV
MLA_SCALE = MLA_QK ** -0.5
GLA_GATE_RANK = 16
GLA_TAU = 16.0
ROPE_BASE = 10000.0
ROPE_FREQS = MLA_ROPE // 4
NORM_EPS = 1e-6
FORGET_MIN = 1e-6
DEEPNORM_ALPHA = (2 * DEPTH) ** 0.25

CHUNK = 64
LANE = 128
SUBLANE = 8
SUB = SUBLANE
NSUB = CHUNK // SUB
REC_BB = 2
LOG2E = 1.4426950408889634
V_ROWS = MLA_DV + 16
HG_COLS = 5 * REC_W
MLA_COLS = 1024
GLA_COLS = 896
IN_COLS = HG_COLS + MLA_COLS + GLA_COLS

VMEM_LIMIT = 56 * 1024 * 1024


def _cparams(sem):
    return pltpu.CompilerParams(dimension_semantics=sem, vmem_limit_bytes=VMEM_LIMIT)


def _dot(a, b):
    return jnp.dot(a, b, preferred_element_type=F32)


def _dot_nt(a, b):
    return lax.dot_general(a, b, (((1,), (1,)), ((), ())), preferred_element_type=F32)


def _dot_tn(a, b):
    return lax.dot_general(a, b, (((0,), (0,)), ((), ())), preferred_element_type=F32)


def _silu(x):
    return x * jax.nn.sigmoid(x)


def _iota_div(shape, axis, div):
    assert div & (div - 1) == 0
    return lax.shift_right_logical(lax.broadcasted_iota(jnp.int32, shape, axis),
                                   jnp.int32(div.bit_length() - 1))


def _mod_kernel(c_ref, w_ref, b_ref, o_ref):
    s = _silu(c_ref[...])
    o_ref[0] = jnp.dot(s, w_ref[0], precision=HIGHEST, preferred_element_type=F32) + b_ref[0]


def _modulation(c_rows, w_mod, b_mod):
    rows = c_rows.shape[0]
    nt = 3 * D_MODEL // 1024
    return pl.pallas_call(
        _mod_kernel,
        grid=(DEPTH, nt),
        in_specs=[pl.BlockSpec((rows, D_MODEL), lambda l, j: (0, 0)),
                  pl.BlockSpec((1, D_MODEL, 1024), lambda l, j: (l, 0, j)),
                  pl.BlockSpec((1, 1, 1024), lambda l, j: (l, 0, j))],
        out_specs=pl.BlockSpec((1, rows, 1024), lambda l, j: (l, 0, j)),
        out_shape=jax.ShapeDtypeStruct((DEPTH, rows, 3 * D_MODEL), F32),
        compiler_params=_cparams(("arbitrary", "arbitrary")),
        name="modulation",
    )(c_rows, w_mod, b_mod.reshape(DEPTH, 1, 3 * D_MODEL))


def _layer_norm(x):
    mu = jnp.mean(x, axis=-1, keepdims=True)
    xc = x - mu
    var = jnp.mean(xc * xc, axis=-1, keepdims=True)
    return xc * lax.rsqrt(var + NORM_EPS)


def _inproj_kernel(x_ref, shift_ref, scale_ref, w_ref, ohg_ref, omla_ref, ogla_ref):
    h = _layer_norm(x_ref[0]) * (1.0 + scale_ref[0]) + shift_ref[0]
    hb = h.astype(BF16)
    ohg_ref[0] = _dot(hb, w_ref[:, 0:HG_COLS])
    omla_ref[0] = _dot(hb, w_ref[:, HG_COLS:HG_COLS + MLA_COLS])
    ogla_ref[0] = _dot(hb, w_ref[:, HG_COLS + MLA_COLS:IN_COLS])


def _in_projection(x, mod, mod_row, w_in_packed, tm):
    B, T, _ = x.shape
    return pl.pallas_call(
        _inproj_kernel,
        grid=(B, T // tm),
        in_specs=[pl.BlockSpec((1, tm, D_MODEL), lambda b, i: (b, i, 0)),
                  pl.BlockSpec((1, 1, D_MODEL), lambda b, i: (mod_row(b), 0, 0)),
                  pl.BlockSpec((1, 1, D_MODEL), lambda b, i: (mod_row(b), 0, 1)),
                  pl.BlockSpec((D_MODEL, IN_COLS), lambda b, i: (0, 0))],
        out_specs=[pl.BlockSpec((1, tm, HG_COLS), lambda b, i: (b, i, 0)),
                   pl.BlockSpec((1, tm, MLA_COLS), lambda b, i: (b, i, 0)),
                   pl.BlockSpec((1, tm, GLA_COLS), lambda b, i: (b, i, 0))],
        out_shape=[jax.ShapeDtypeStruct((B, T, HG_COLS), F32),
                   jax.ShapeDtypeStruct((B, T, MLA_COLS), F32),
                   jax.ShapeDtypeStruct((B, T, GLA_COLS), F32)],
        compiler_params=_cparams(("arbitrary", "arbitrary")),
        name="in_projection",
    )(x, mod, mod, w_in_packed)


def _split3(g):
    hi = g.astype(BF16)
    r1 = g - hi.astype(F32)
    mid = r1.astype(BF16)
    lo = (r1 - mid.astype(F32)).astype(BF16)
    return hi, mid, lo


def _rec_chunk(q, k, v, g, st_ref, a_loc, head_ones, rev, dk):
    dkt = HEADS * dk
    ghi, gmid, glo = _split3(g * LOG2E)
    lcum = _dot(a_loc, ghi) + _dot(a_loc, gmid) + _dot(a_loc, glo)

    phys = [(NSUB - 1 - p) if rev else p for p in range(NSUB)]
    rows = lambda I: slice(I * SUB, (I + 1) * SUB)
    last_row = [(I * SUB) if rev else (I * SUB + SUB - 1) for I in phys]
    ltot = [jnp.broadcast_to(lcum[r:r + 1, :], (SUB, dkt)) for r in last_row]

    q_in, k_out = [], []
    for p, I in enumerate(phys):
        lc = lcum[rows(I), :]
        q_in.append(q[rows(I), :] * jnp.exp2(lc))
        k_out.append(k[rows(I), :] * jnp.exp2(ltot[p] - lc))

    def running(order):
        sums, acc = {}, None
        for p in order:
            sums[p] = acc
            acc = ltot[p] if acc is None else acc + ltot[p]
        return sums, acc

    pre, total = running(range(NSUB))
    suf, _ = running(reversed(range(NSUB)))
    scaled = lambda x, e: x if e is None else x * jnp.exp2(e)
    q_hat = [scaled(q_in[p], pre[p]) for p in range(NSUB)]
    k_hat = [scaled(k_out[p], suf[p]) for p in range(NSUB)]

    def phys_concat(parts):
        order = sorted(range(NSUB), key=lambda p: phys[p])
        return jnp.concatenate([parts[p] for p in order], axis=0)

    lhs, lhs_at = [], {}
    for p in range(1, NSUB):
        gap = None
        for dist in range(1, p + 1):
            lhs_at[(p, dist)] = len(lhs)
            lhs.append(scaled(q_in[p], gap))
            gap = ltot[p - dist] if gap is None else gap + ltot[p - dist]
    k_out_all = phys_concat(k_out).astype(BF16)
    rk = _iota_div((HEADS * CHUNK, dkt), 0, CHUNK)
    ck = _iota_div((HEADS * CHUNK, dkt), 1, dk)
    kbd_t = jnp.where(rk == ck, jnp.concatenate([k_out_all] * HEADS, axis=0), 0)
    sc = _dot_nt(jnp.concatenate(lhs, axis=0).astype(BF16), kbd_t)

    ri = lax.broadcasted_iota(jnp.int32, (SUB, dkt), 0)
    units = []
    for I in range(NSUB):
        q_i = q[rows(I), :]
        l_i = lcum[rows(I), :]
        for jj in range(SUB):
            r = I * SUB + jj
            diff = l_i - lcum[r:r + 1, :]
            if jj != (SUB - 1 if rev else 0):
                diff = jnp.where((ri <= jj) if rev else (ri >= jj), diff, -1e30)
            units.append(q_i * k[r:r + 1, :] * jnp.exp2(diff))
    w_all = _dot(jnp.concatenate(units, axis=0).astype(BF16), head_ones)

    lane = lax.broadcasted_iota(jnp.int32, (SUB, HEADS * CHUNK), 1)
    col_blk = lax.shift_right_logical(lane, jnp.int32(SUB.bit_length() - 1)) & (NSUB - 1)
    col_p = (NSUB - 1 - col_blk) if rev else col_blk
    bits = [(lane & (1 << b)) != 0 for b in range(SUB.bit_length() - 1)]
    p_rows = []
    for p, I in enumerate(phys):
        level = [w_all[(I * SUB + jj) * SUB:(I * SUB + jj + 1) * SUB, :] for jj in range(SUB)]
        for bit in bits:
            level = [jnp.where(bit, level[2 * n + 1], level[2 * n]) for n in range(len(level) // 2)]
        acc_p = jnp.where(col_p == p, level[0], 0.0)
        for dist in range(1, p + 1):
            n = lhs_at[(p, dist)]
            acc_p = jnp.where(col_p == p - dist, sc[n * SUB:(n + 1) * SUB, :], acc_p)
        p_rows.append(acc_p)
    p_all = phys_concat(p_rows).astype(BF16)
    rv = _iota_div((HEADS * CHUNK, REC_W), 0, CHUNK)
    cv = _iota_div((HEADS * CHUNK, REC_W), 1, REC_DV)
    vb = v.astype(BF16)
    vbd = jnp.where(rv == cv, jnp.concatenate([vb] * HEADS, axis=0), 0)
    o = _dot(p_all, vbd)

    st = st_ref[...]
    o = o + _dot_nt(phys_concat(q_hat).astype(BF16), st.astype(BF16))
    rs = _iota_div((REC_W, dkt), 0, REC_DV)
    cs = _iota_div((REC_W, dkt), 1, dk)
    upd = _dot_tn(vb, phys_concat(k_hat).astype(BF16))
    st_ref[...] = st * jnp.exp2(total[0:1, :]) + jnp.where(rs == cs, upd, 0.0)
    return o


def _hg_prep(z, lb, d):
    q = _silu(z[:, 0:REC_W])
    v = z[:, 3 * REC_W:4 * REC_W]
    zz = z[:, (1 + d) * REC_W:(2 + d) * REC_W]
    f = lb + (1.0 - lb) * jax.nn.sigmoid(zz)
    g = jnp.log(jnp.maximum(f, FORGET_MIN))
    k = (1.0 - lb) * jax.nn.sigmoid(-zz)
    return q, k, g, v


def _log_sigmoid(x):
    return jnp.minimum(x, 0.0) - jnp.log(1.0 + jnp.exp(-jnp.abs(x)))


def _gla_prep(z, wg, bg):
    dkt = HEADS * GLA_DK
    q = z[:, 0:dkt] * (GLA_DK ** -0.5)
    k = z[:, dkt:2 * dkt]
    v = z[:, 2 * dkt:2 * dkt + REC_W]
    za = z[:, GLA_COLS - LANE:GLA_COLS].astype(BF16)
    g = _log_sigmoid(_dot(za, wg) + bg) * (1.0 / GLA_TAU)
    return q, k, g, v


def _rec_kernel(*refs, mixer, dk, bb):
    if mixer == "hg":
        (zf_ref, zb_ref, lb_ref, aloc_ref, ones_ref, sf0_ref, sb0_ref,
         of_ref, ob_ref, sf1_ref, sb1_ref, stf, stb) = refs
    else:
        (zf_ref, zb_ref, wg_ref, bg_ref, aloc_ref, ones_ref, sf0_ref, sb0_ref,
         of_ref, ob_ref, sf1_ref, sb1_ref, stf, stb) = refs
    c = pl.program_id(1)

    @pl.when(c == 0)
    def _():
        stf[...] = sf0_ref[...]
        stb[...] = sb0_ref[...]

    head_ones = ones_ref[...]
    for i in range(bb):
        for d, (z_ref, o_ref, st) in enumerate(((zf_ref, of_ref, stf), (zb_ref, ob_ref, stb))):
            z = z_ref[i]
            if mixer == "hg":
                q, k, g, v = _hg_prep(z, lb_ref[d:d + 1, :], d)
            else:
                q, k, g, v = _gla_prep(z, wg_ref[d], bg_ref[d])
            o_ref[i] = _rec_chunk(q, k, v, g, st.at[i], aloc_ref[d], head_ones, rev=(d == 1), dk=dk)

    @pl.when(c == pl.num_programs(1) - 1)
    def _():
        sf1_ref[...] = stf[...]
        sb1_ref[...] = stb[...]


def _rec_constants(dk):
    a = np.zeros((2, CHUNK, CHUNK), np.float32)
    for i in range(CHUNK):
        for j in range(CHUNK):
            if i // SUB == j // SUB:
                a[0, i, j] = 1.0 if j <= i else 0.0
                a[1, i, j] = 1.0 if j >= i else 0.0
    dkt = HEADS * dk
    ones = np.zeros((dkt, REC_W), np.float32)
    for r in range(dkt):
        ones[r, (r // dk) * REC_DV:(r // dk + 1) * REC_DV] = 1.0
    return jnp.asarray(a, BF16), jnp.asarray(ones, BF16)


def _recurrence(z, mixer, params, s0f, s0b):
    B, T, cols = z.shape
    dk = HG_DK if mixer == "hg" else GLA_DK
    dkt = HEADS * dk
    nc = T // CHUNK
    bb = REC_BB if B % REC_BB == 0 else 1
    a_loc, head_ones = _rec_constants(dk)
    full = lambda shape: pl.BlockSpec(shape, lambda b, c: (0,) * len(shape))
    in_specs = [pl.BlockSpec((bb, CHUNK, cols), lambda b, c: (b, c, 0)),
                pl.BlockSpec((bb, CHUNK, cols), lambda b, c: (b, nc - 1 - c, 0))]
    in_specs += [full(p.shape) for p in params]
    in_specs += [full(a_loc.shape), full(head_ones.shape),
                 pl.BlockSpec((bb, REC_W, dkt), lambda b, c: (b, 0, 0)),
                 pl.BlockSpec((bb, REC_W, dkt), lambda b, c: (b, 0, 0))]
    out_specs = [pl.BlockSpec((bb, CHUNK, REC_W), lambda b, c: (b, c, 0)),
                 pl.BlockSpec((bb, CHUNK, REC_W), lambda b, c: (b, nc - 1 - c, 0)),
                 pl.BlockSpec((bb, REC_W, dkt), lambda b, c: (b, 0, 0)),
                 pl.BlockSpec((bb, REC_W, dkt), lambda b, c: (b, 0, 0))]
    out_shape = [jax.ShapeDtypeStruct((B, T, REC_W), F32)] * 2 + [jax.ShapeDtypeStruct((B, REC_W, dkt), F32)] * 2
    return pl.pallas_call(
        functools.partial(_rec_kernel, mixer=mixer, dk=dk, bb=bb),
        grid=(B // bb, nc),
        in_specs=in_specs,
        out_specs=out_specs,
        out_shape=out_shape,
        scratch_shapes=[pltpu.VMEM((bb, REC_W, dkt), F32), pltpu.VMEM((bb, REC_W, dkt), F32)],
        compiler_params=_cparams(("arbitrary", "arbitrary")),
        name="recurrence_" + mixer,
    )(z, z, *params, a_loc, head_ones, s0f, s0b)


def _rms_norm(x, g):
    return x * lax.rsqrt(jnp.mean(x * x, axis=-1, keepdims=True) + NORM_EPS) * g


def _swap_halves(x):
    n = x.shape[-1]
    lane = lax.broadcasted_iota(jnp.int32, x.shape, x.ndim - 1)
    first_half = (lane & (2 * ROPE_FREQS - 1)) < ROPE_FREQS
    return jnp.where(first_half, pltpu.roll(x, n - ROPE_FREQS, x.ndim - 1),
                     pltpu.roll(x, ROPE_FREQS, x.ndim - 1))


def _mla_up_kernel(z_ref, qg_ref, kvg_ref, wuq_ref, wuk_ref, wvt_ref, cos_ref, sin_ref,
                   q_ref, k_ref, vt_ref, *, rope):
    z = z_ref[0]
    cq = _rms_norm(z[:, 0:MLA_Q_RANK], qg_ref[...]).astype(BF16)
    ckv = _rms_norm(z[:, MLA_Q_RANK:MLA_Q_RANK + MLA_KV_RANK], kvg_ref[...]).astype(BF16)
    kr = z[:, MLA_Q_RANK + MLA_KV_RANK:MLA_Q_RANK + MLA_KV_RANK + LANE]
    qf = _dot(cq, wuq_ref[...]) * (MLA_SCALE * LOG2E)
    kf = _dot(ckv, wuk_ref[...])
    vt = _dot_nt(wvt_ref[...], ckv)
    q_rope = qf[:, HEADS * MLA_NOPE:]
    if rope:
        cos = cos_ref[...]
        sin = sin_ref[...]
        cos_q = jnp.concatenate([cos, cos], axis=-1)
        sin_q = jnp.concatenate([sin, sin], axis=-1)
        q_rope = q_rope * cos_q + _swap_halves(q_rope) * sin_q
        kr = kr * cos + _swap_halves(kr) * sin
    kr = kr[:, 0:MLA_ROPE]
    ones = jnp.ones((V_ROWS - MLA_DV, vt.shape[1]), BF16)
    for h in range(HEADS):
        q_ref[0, h, :, 0:MLA_NOPE] = qf[:, h * MLA_NOPE:(h + 1) * MLA_NOPE].astype(BF16)
        q_ref[0, h, :, MLA_NOPE:MLA_QK] = q_rope[:, h * MLA_ROPE:(h + 1) * MLA_ROPE].astype(BF16)
        k_ref[0, h, :, 0:MLA_NOPE] = kf[:, h * MLA_NOPE:(h + 1) * MLA_NOPE].astype(BF16)
        k_ref[0, h, :, MLA_NOPE:MLA_QK] = kr.astype(BF16)
        vt_ref[0, h, 0:MLA_DV, :] = vt[h * MLA_DV:(h + 1) * MLA_DV, :].astype(BF16)
        vt_ref[0, h, MLA_DV:V_ROWS, :] = ones


def _mla_up(z_mla, q_norm_g, kv_norm_g, wuq_packed, wuk_packed, wvt_packed, cos_t, sin_t, rope, tm):
    B, T, _ = z_mla.shape
    full = lambda shape: pl.BlockSpec(shape, lambda b, i: (0,) * len(shape))
    return pl.pallas_call(
        functools.partial(_mla_up_kernel, rope=rope),
        grid=(B, T // tm),
        in_specs=[pl.BlockSpec((1, tm, MLA_COLS), lambda b, i: (b, i, 0)),
                  full((1, MLA_Q_RANK)), full((1, MLA_KV_RANK)),
                  full(wuq_packed.shape), full(wuk_packed.shape), full(wvt_packed.shape),
                  pl.BlockSpec((tm, LANE), lambda b, i: (i, 0)),
                  pl.BlockSpec((tm, LANE), lambda b, i: (i, 0))],
        out_specs=[pl.BlockSpec((1, HEADS, tm, MLA_QK), lambda b, i: (b, 0, i, 0)),
                   pl.BlockSpec((1, HEADS, tm, MLA_QK), lambda b, i: (b, 0, i, 0)),
                   pl.BlockSpec((1, HEADS, V_ROWS, tm), lambda b, i: (b, 0, 0, i))],
        out_shape=[jax.ShapeDtypeStruct((B, HEADS, T, MLA_QK), BF16),
                   jax.ShapeDtypeStruct((B, HEADS, T, MLA_QK), BF16),
                   jax.ShapeDtypeStruct((B, HEADS, V_ROWS, T), BF16)],
        compiler_params=_cparams(("arbitrary", "arbitrary")),
        name="mla_up_rope" if rope else "mla_up",
    )(z_mla, q_norm_g.reshape(1, -1), kv_norm_g.reshape(1, -1), wuq_packed, wuk_packed, wvt_packed,
      cos_t, sin_t)


def _attn_stage(q_ref, k_ref, vt_ref, o_ref, s_w, m_w, s_r, m_r):
    n_keys = k_ref.shape[2]
    half = n_keys // 2
    parts = [slice(0, half), slice(half, n_keys)]
    q = q_ref[0, 0]
    m = None
    for ks in parts:
        s = _dot_nt(k_ref[0, 0, ks, :], q)
        s_w[ks, :] = s
        mx = jnp.max(s, axis=0, keepdims=True)
        m = mx if m is None else jnp.maximum(m, mx)
    m_w[0:1, :] = m
    m_prev = m_r[0:1, :]
    acc = None
    for ks in parts:
        part = _dot(vt_ref[0, 0, :, ks], jnp.exp2(s_r[ks, :] - m_prev).astype(BF16))
        acc = part if acc is None else acc + part
    o_t = acc[0:MLA_DV, :] * (1.0 / acc[MLA_DV:MLA_DV + 1, :])
    o_ref[0] = o_t.T


def _attn_kernel(q_ref, k_ref, vt_ref, o_ref, s_a, m_a, s_b, m_b):
    t = pl.program_id(0)

    @pl.when(t == 0)
    def _():
        s_b[...] = jnp.zeros_like(s_b)
        m_b[...] = jnp.zeros_like(m_b)

    @pl.when(t % 2 == 0)
    def _():
        _attn_stage(q_ref, k_ref, vt_ref, o_ref, s_a, m_a, s_b, m_b)

    @pl.when(t % 2 == 1)
    def _():
        _attn_stage(q_ref, k_ref, vt_ref, o_ref, s_b, m_b, s_a, m_a)


def _attention(q, k, vt, tq):
    B, H, N, _ = q.shape
    M = k.shape[2]
    assert M % (2 * LANE) == 0
    nq = N // tq
    n_tiles = B * H * nq

    def tile(t):
        return t // (H * nq), (t // nq) % H, t % nq

    def score_tile(t):
        return tile(jnp.minimum(t, n_tiles - 1))

    def finish_tile(t):
        return tile(jnp.maximum(t - 1, 0))

    def q_map(t):
        b, h, i = score_tile(t)
        return b, h, i, 0

    def k_map(t):
        b, h, _ = score_tile(t)
        return b, h, 0, 0

    def vt_map(t):
        b, h, _ = finish_tile(t)
        return b, h, 0, 0

    def o_map(t):
        b, h, i = finish_tile(t)
        return b, i, h

    return pl.pallas_call(
        _attn_kernel,
        grid=(n_tiles + 1,),
        in_specs=[pl.BlockSpec((1, 1, tq, MLA_QK), q_map),
                  pl.BlockSpec((1, 1, M, MLA_QK), k_map),
                  pl.BlockSpec((1, 1, V_ROWS, M), vt_map)],
        out_specs=pl.BlockSpec((1, tq, MLA_DV), o_map),
        out_shape=jax.ShapeDtypeStruct((B, N, H * MLA_DV), F32),
        scratch_shapes=[pltpu.VMEM((M, tq), F32), pltpu.VMEM((SUBLANE, tq), F32),
                        pltpu.VMEM((M, tq), F32), pltpu.VMEM((SUBLANE, tq), F32)],
        compiler_params=_cparams(("arbitrary",)),
        name="attention",
    )(q, k, vt)


def _head_rms(o, ones_bf, g):
    sq = o * o
    hi = sq.astype(BF16)
    lo = (sq - hi.astype(F32)).astype(BF16)
    ms = (_dot(hi, ones_bf) + _dot(lo, ones_bf)) * (1.0 / REC_DV)
    return o * lax.rsqrt(ms + NORM_EPS) * g


def _out_kernel(x_ref, gate_ref, zhg_ref, zmla_ref, zgla_ref, hgf_ref, hgb_ref, att_ref, glf_ref, glb_ref,
                ones_ref, hgn_ref, gln_ref, w_ref, lng_ref, lnb_ref, o_ref):
    ones_bf = ones_ref[...]
    y_hg = _head_rms(hgf_ref[0] + hgb_ref[0], ones_bf, hgn_ref[...]) * _silu(zhg_ref[0])
    y_mla = att_ref[0] * _silu(zmla_ref[0])
    y_gla = _head_rms(glf_ref[0] + glb_ref[0], ones_bf, gln_ref[...]) * _silu(zgla_ref[0])
    proj = (_dot(y_hg.astype(BF16), w_ref[0:REC_W, :])
            + _dot(y_mla.astype(BF16), w_ref[REC_W:REC_W + MLA_W, :])
            + _dot(y_gla.astype(BF16), w_ref[REC_W + MLA_W:, :]))
    r = DEEPNORM_ALPHA * x_ref[0] + gate_ref[0] * proj
    o_ref[0] = _layer_norm(r) * lng_ref[...] + lnb_ref[...]


def _output_block(x, mod, mod_row, z_hg, z_mla, z_gla, hg_f, hg_b, att, gl_f, gl_b,
                  hg_norm_g, gla_norm_g, w_out_bf, ln_g, ln_b, tm):
    B, T, _ = x.shape
    ones = np.zeros((REC_W, REC_W), np.float32)
    for r in range(REC_W):
        ones[r, (r // REC_DV) * REC_DV:(r // REC_DV + 1) * REC_DV] = 1.0
    ones = jnp.asarray(ones, BF16)
    row = lambda w: pl.BlockSpec((1, tm, w), lambda b, i: (b, i, 0))
    full = lambda shape: pl.BlockSpec(shape, lambda b, i: (0,) * len(shape))
    return pl.pallas_call(
        _out_kernel,
        grid=(B, T // tm),
        in_specs=[row(D_MODEL),
                  pl.BlockSpec((1, 1, D_MODEL), lambda b, i: (mod_row(b), 0, 2)),
                  pl.BlockSpec((1, tm, REC_W), lambda b, i: (b, i, HG_COLS // REC_W - 1)),
                  pl.BlockSpec((1, tm, MLA_W), lambda b, i: (b, i, 1)),
                  pl.BlockSpec((1, tm, REC_W), lambda b, i: (b, i, 2)),
                  row(REC_W), row(REC_W), row(MLA_W), row(REC_W), row(REC_W),
                  full((REC_W, REC_W)), full((1, REC_W)), full((1, REC_W)),
                  full((D_MODEL, D_MODEL)), full((1, D_MODEL)), full((1, D_MODEL))],
        out_specs=row(D_MODEL),
        out_shape=jax.ShapeDtypeStruct((B, T, D_MODEL), F32),
        compiler_params=_cparams(("arbitrary", "arbitrary")),
        name="output_block",
    )(x, mod, z_hg, z_mla, z_gla, hg_f, hg_b, att, gl_f, gl_b, ones,
      jnp.tile(hg_norm_g, HEADS).reshape(1, REC_W), jnp.tile(gla_norm_g, HEADS).reshape(1, REC_W),
      w_out_bf, ln_g.reshape(1, D_MODEL), ln_b.reshape(1, D_MODEL))


def _pack_w_in(w):
    o = np.cumsum([0, 256, 256, 256, 256, 256, 256, 128, 64, 512, 128, 128, 256, 16, 16, 256])
    col = lambda i: w[:, int(o[i]):int(o[i + 1])]
    zeros = lambda n: jnp.zeros((w.shape[0], n), w.dtype)
    hg = [col(0), col(1), col(2), col(3), col(4)]
    mla = [col(5), col(6), col(7), zeros(64), col(8)]
    gla = [col(9), col(10), col(11), col(14), col(12), col(13), zeros(96)]
    return jnp.concatenate(hg + mla + gla, axis=1).astype(BF16)


def _pack_wuq(w):
    w4 = w.reshape(MLA_Q_RANK, HEADS, MLA_QK)
    return jnp.concatenate([w4[:, :, :MLA_NOPE].reshape(MLA_Q_RANK, -1),
                            w4[:, :, MLA_NOPE:].reshape(MLA_Q_RANK, -1)], axis=1).astype(BF16)


def _pack_wukv(w):
    w4 = w.reshape(MLA_KV_RANK, HEADS, MLA_NOPE + MLA_DV)
    wk = w4[:, :, :MLA_NOPE].reshape(MLA_KV_RANK, -1).astype(BF16)
    wvt = w4[:, :, MLA_NOPE:].reshape(MLA_KV_RANK, -1).T.astype(BF16)
    return wk, wvt


def _pack_gla_gate(w_a2, b_a):
    dkt = HEADS * GLA_DK
    wg = jnp.zeros((2, LANE, dkt), F32)
    wg = wg.at[0, 0:GLA_GATE_RANK].set(w_a2[0])
    wg = wg.at[1, GLA_GATE_RANK:2 * GLA_GATE_RANK].set(w_a2[1])
    return wg.astype(BF16), b_a.reshape(2, 1, dkt)


def _rope_tables(n):
    rows = n // GRID_W
    pos_r = jnp.repeat(jnp.arange(rows, dtype=F32), GRID_W)
    pos_c = jnp.tile(jnp.arange(GRID_W, dtype=F32), rows)
    inv = 1.0 / (ROPE_BASE ** (jnp.arange(ROPE_FREQS, dtype=F32) / ROPE_FREQS))
    ang = jnp.stack([pos_r, pos_c], axis=-1)[:, :, None] * inv
    cos = jnp.cos(ang)
    sin = jnp.sin(ang)
    cos64 = jnp.concatenate([cos, cos], axis=-1).reshape(n, MLA_ROPE)
    sin64 = jnp.concatenate([-sin, sin], axis=-1).reshape(n, MLA_ROPE)
    return jnp.tile(cos64, (1, LANE // MLA_ROPE)), jnp.tile(sin64, (1, LANE // MLA_ROPE))


def _pick_tile(t, pref):
    while t % pref:
        pref //= 2
    return pref


def kernel(x, c, ctx, c_ctx, w_mod, b_mod, w_in, w_out, ln_g, ln_b, hg_lb_logits, hg_norm_g,
           mla_q_norm_g, mla_kv_norm_g, mla_w_uq, mla_w_ukv, gla_w_a2, gla_b_a, gla_norm_g):
    B, N, _ = x.shape
    Lc = ctx.shape[1]
    cos_t, sin_t = _rope_tables(N)
    cos_c = jnp.zeros((Lc, LANE), F32)

    lb_soft = jax.nn.softmax(hg_lb_logits.astype(F32), axis=0)
    hg_lb = jnp.clip(jnp.cumsum(lb_soft, axis=0) - lb_soft[0:1], 0.0, 1.0)

    n_rows = -(-(B + 1) // 8) * 8
    c_rows = jnp.zeros((n_rows, D_MODEL), F32).at[0:B].set(c).at[B].set(c_ctx)
    mods = _modulation(c_rows, w_mod, b_mod)

    lat_row = lambda b: b
    ctx_row = lambda b: B
    tm_l, tm_c = _pick_tile(N, 256), _pick_tile(Lc, 256)
    zeros_state = lambda dkt: jnp.zeros((B, REC_W, dkt), F32)

    xc = ctx
    for l in range(DEPTH):
        need_ctx = l < DEPTH - 1
        mod = mods[l].reshape(n_rows, 1, 3 * D_MODEL)
        w_in_p = _pack_w_in(w_in[l])
        zl_hg, zl_mla, zl_gla = _in_projection(x, mod, lat_row, w_in_p, tm_l)
        zc_hg, zc_mla, zc_gla = _in_projection(xc, mod, ctx_row, w_in_p, tm_c)

        hg_params = (hg_lb[l],)
        hc_f, hc_b, s_f, s_b = _recurrence(zc_hg, "hg", hg_params, zeros_state(HEADS * HG_DK),
                                           zeros_state(HEADS * HG_DK))
        hl_f, hl_b, _, _ = _recurrence(zl_hg, "hg", hg_params, s_f, s_b)
        gla_params = _pack_gla_gate(gla_w_a2[l], gla_b_a[l])
        gc_f, gc_b, s_f, s_b = _recurrence(zc_gla, "gla", gla_params, zeros_state(HEADS * GLA_DK),
                                           zeros_state(HEADS * GLA_DK))
        gl_f, gl_b, _, _ = _recurrence(zl_gla, "gla", gla_params, s_f, s_b)

        wuq_p = _pack_wuq(mla_w_uq[l])
        wuk_p, wvt_p = _pack_wukv(mla_w_ukv[l])
        q_l, k_l, vt_l = _mla_up(zl_mla, mla_q_norm_g[l], mla_kv_norm_g[l], wuq_p, wuk_p, wvt_p,
                                 cos_t, sin_t, True, tm_l)
        q_c, k_c, vt_c = _mla_up(zc_mla, mla_q_norm_g[l], mla_kv_norm_g[l], wuq_p, wuk_p, wvt_p,
                                 cos_c, cos_c, False, tm_c)
        att_l = _attention(q_l, jnp.concatenate([k_c, k_l], axis=2),
                           jnp.concatenate([vt_c, vt_l], axis=3), _pick_tile(N, 256))

        w_out_bf = w_out[l].astype(BF16)
        x_new = _output_block(x, mod, lat_row, zl_hg, zl_mla, zl_gla, hl_f, hl_b, att_l, gl_f, gl_b,
                              hg_norm_g[l], gla_norm_g[l], w_out_bf, ln_g[l], ln_b[l], tm_l)
        if need_ctx:
            att_c = _attention(q_c, k_c, vt_c, _pick_tile(Lc, 256))
            xc = _output_block(xc, mod, ctx_row, zc_hg, zc_mla, zc_gla, hc_f, hc_b, att_c, gc_f, gc_b,
                               hg_norm_g[l], gla_norm_g[l], w_out_bf, ln_g[l], ln_b[l], tm_c)
        x = x_new
    return x
```

```python
import functools
import inspect

import numpy as np
import jax
import jax.numpy as jnp
from jax import lax
from jax.experimental import pallas as pl
from jax.experimental.pallas import tpu as pltpu

F32 = jnp.float32
BF16 = jnp.bfloat16
HIGHEST = lax.Precision.HIGHEST

D_MODEL = 1024
DEPTH = 2
GRID_W = 64
HEADS = 4
HG_DK = 64
GLA_DK = 32
REC_DV = 64
REC_W = HEADS * REC_DV
MLA_NOPE = 128
MLA_ROPE = 64
MLA_DV = 128
MLA_Q_RANK = 256
MLA_KV_RANK = 128
MLA_QK = MLA_NOPE + MLA_ROPE
MLA_W = HEADS * MLA_DV
MLA_SCALE = MLA_QK ** -0.5
GLA_GATE_RANK = 16
GLA_TAU = 16.0
ROPE_BASE = 10000.0
ROPE_FREQS = MLA_ROPE // 4
NORM_EPS = 1e-6
FORGET_MIN = 1e-6
DEEPNORM_ALPHA = (2 * DEPTH) ** 0.25

CHUNK = 64
LANE = 128
SUBLANE = 8
SUB = SUBLANE
NSUB = CHUNK // SUB
REC_BB = 8
LOG2E = 1.4426950408889634
V_ROWS = MLA_DV + 16
HG_COLS = 5 * REC_W
MLA_COLS = 1024
GLA_COLS = 896
IN_COLS = HG_COLS + MLA_COLS + GLA_COLS

VMEM_LIMIT = 56 * 1024 * 1024


def _cparams(sem):
    return pltpu.CompilerParams(dimension_semantics=sem, vmem_limit_bytes=VMEM_LIMIT)


def _dot(a, b):
    return jnp.dot(a, b, preferred_element_type=F32)


def _dot_nt(a, b):
    return lax.dot_general(a, b, (((1,), (1,)), ((), ())), preferred_element_type=F32)


def _dot_tn(a, b):
    return lax.dot_general(a, b, (((0,), (0,)), ((), ())), preferred_element_type=F32)


def _silu(x):
    return x * jax.nn.sigmoid(x)


def _iota_div(shape, axis, div):
    assert div & (div - 1) == 0
    return lax.shift_right_logical(lax.broadcasted_iota(jnp.int32, shape, axis),
                                   jnp.int32(div.bit_length() - 1))


def _mod_kernel(c_ref, w_ref, b_ref, o_ref):
    s = _silu(c_ref[...])
    o_ref[0] = jnp.dot(s, w_ref[0], precision=HIGHEST, preferred_element_type=F32) + b_ref[0]


def _modulation(c_rows, w_mod, b_mod):
    rows = c_rows.shape[0]
    nt = 3 * D_MODEL // 1024
    return pl.pallas_call(
        _mod_kernel,
        grid=(DEPTH, nt),
        in_specs=[pl.BlockSpec((rows, D_MODEL), lambda l, j: (0, 0)),
                  pl.BlockSpec((1, D_MODEL, 1024), lambda l, j: (l, 0, j)),
                  pl.BlockSpec((1, 1, 1024), lambda l, j: (l, 0, j))],
        out_specs=pl.BlockSpec((1, rows, 1024), lambda l, j: (l, 0, j)),
        out_shape=jax.ShapeDtypeStruct((DEPTH, rows, 3 * D_MODEL), F32),
        compiler_params=_cparams(("arbitrary", "arbitrary")),
        name="modulation",
    )(c_rows, w_mod, b_mod.reshape(DEPTH, 1, 3 * D_MODEL))


def _layer_norm(x):
    mu = jnp.mean(x, axis=-1, keepdims=True)
    xc = x - mu
    var = jnp.mean(xc * xc, axis=-1, keepdims=True)
    return xc * lax.rsqrt(var + NORM_EPS)


def _inproj_kernel(x_ref, shift_ref, scale_ref, w_ref, ohg_ref, omla_ref, ogla_ref):
    h = _layer_norm(x_ref[0]) * (1.0 + scale_ref[0]) + shift_ref[0]
    hb = h.astype(BF16)
    ohg_ref[0] = _dot(hb, w_ref[:, 0:HG_COLS])
    omla_ref[0] = _dot(hb, w_ref[:, HG_COLS:HG_COLS + MLA_COLS])
    ogla_ref[0] = _dot(hb, w_ref[:, HG_COLS + MLA_COLS:IN_COLS])


def _in_projection(x, mod, mod_row, w_in_packed, tm):
    B, T, _ = x.shape
    return pl.pallas_call(
        _inproj_kernel,
        grid=(B, T // tm),
        in_specs=[pl.BlockSpec((1, tm, D_MODEL), lambda b, i: (b, i, 0)),
                  pl.BlockSpec((1, 1, D_MODEL), lambda b, i: (mod_row(b), 0, 0)),
                  pl.BlockSpec((1, 1, D_MODEL), lambda b, i: (mod_row(b), 0, 1)),
                  pl.BlockSpec((D_MODEL, IN_COLS), lambda b, i: (0, 0))],
        out_specs=[pl.BlockSpec((1, tm, HG_COLS), lambda b, i: (b, i, 0)),
                   pl.BlockSpec((1, tm, MLA_COLS), lambda b, i: (b, i, 0)),
                   pl.BlockSpec((1, tm, GLA_COLS), lambda b, i: (b, i, 0))],
        out_shape=[jax.ShapeDtypeStruct((B, T, HG_COLS), F32),
                   jax.ShapeDtypeStruct((B, T, MLA_COLS), F32),
                   jax.ShapeDtypeStruct((B, T, GLA_COLS), F32)],
        compiler_params=_cparams(("arbitrary", "arbitrary")),
        name="in_projection",
    )(x, mod, mod, w_in_packed)


def _split3(g):
    hi = g.astype(BF16)
    r1 = g - hi.astype(F32)
    mid = r1.astype(BF16)
    lo = (r1 - mid.astype(F32)).astype(BF16)
    return hi, mid, lo


def _rec_chain(prep, store_out, st_ref, a_loc, head_ones, rev, dk, min_step_log2):
    dkt = HEADS * dk
    inputs = prep()
    q, k, g, v = (yield from inputs) if inspect.isgenerator(inputs) else inputs
    ghi, gmid, glo = _split3(g * LOG2E)
    lcum = _dot(a_loc, ghi) + _dot(a_loc, gmid) + _dot(a_loc, glo)
    yield

    phys = [(NSUB - 1 - p) if rev else p for p in range(NSUB)]
    rows = lambda I: slice(I * SUB, (I + 1) * SUB)
    last_row = [(I * SUB) if rev else (I * SUB + SUB - 1) for I in phys]
    ltot = [jnp.broadcast_to(lcum[r:r + 1, :], (SUB, dkt)) for r in last_row]

    q_in, k_out = [], []
    for p, I in enumerate(phys):
        lc = lcum[rows(I), :]
        q_in.append(q[rows(I), :] * jnp.exp2(lc))
        k_out.append(k[rows(I), :] * jnp.exp2(ltot[p] - lc))

    def running(order):
        sums, acc = {}, None
        for p in order:
            sums[p] = acc
            acc = ltot[p] if acc is None else acc + ltot[p]
        return sums, acc

    pre, total = running(range(NSUB))
    suf, _ = running(reversed(range(NSUB)))
    scaled = lambda x, e: x if e is None else x * jnp.exp2(e)
    q_hat = [scaled(q_in[p], pre[p]) for p in range(NSUB)]
    k_hat = [scaled(k_out[p], suf[p]) for p in range(NSUB)]

    def phys_concat(parts):
        order = sorted(range(NSUB), key=lambda p: phys[p])
        return jnp.concatenate([parts[p] for p in order], axis=0)

    lhs, lhs_at = [], {}
    for p in range(1, NSUB):
        gap = None
        for dist in range(1, p + 1):
            lhs_at[(p, dist)] = len(lhs)
            lhs.append(scaled(q_in[p], gap))
            gap = ltot[p - dist] if gap is None else gap + ltot[p - dist]
    k_out_all = phys_concat(k_out).astype(BF16)
    rk = _iota_div((HEADS * CHUNK, dkt), 0, CHUNK)
    ck = _iota_div((HEADS * CHUNK, dkt), 1, dk)
    kbd_t = jnp.where(rk == ck, jnp.concatenate([k_out_all] * HEADS, axis=0), 0)
    sc = _dot_nt(jnp.concatenate(lhs, axis=0).astype(BF16), kbd_t)

    if min_step_log2 is None:
        ri = lax.broadcasted_iota(jnp.int32, (SUB, dkt), 0)
        units = []
        for I in range(NSUB):
            q_i = q[rows(I), :]
            l_i = lcum[rows(I), :]
            for jj in range(SUB):
                r = I * SUB + jj
                diff = l_i - lcum[r:r + 1, :]
                if jj != (SUB - 1 if rev else 0):
                    diff = jnp.where((ri <= jj) if rev else (ri >= jj), diff, -1e30)
                units.append(q_i * k[r:r + 1, :] * jnp.exp2(diff))
        w_all = _dot(jnp.concatenate(units, axis=0).astype(BF16), head_ones)
    else:
        assert (SUB // 2) * -min_step_log2 < 120
        q_mid, k_mid = [], []
        for p, I in enumerate(phys):
            m = I * SUB + (SUB // 2 if rev else SUB // 2 - 1)
            rel = lcum[rows(I), :] - jnp.broadcast_to(lcum[m:m + 1, :], (SUB, dkt))
            q_mid.append(q[rows(I), :] * jnp.exp2(rel))
            k_mid.append(k[rows(I), :] * jnp.exp2(-rel))
        k_mid_all = phys_concat(k_mid).astype(BF16)
        kbd_mid_t = jnp.where(rk == ck, jnp.concatenate([k_mid_all] * HEADS, axis=0), 0)
        sc_diag = _dot_nt(phys_concat(q_mid).astype(BF16), kbd_mid_t)

    st = st_ref[...]
    vb = v.astype(BF16)
    o_inter = _dot_nt(phys_concat(q_hat).astype(BF16), st.astype(BF16))
    upd = _dot_tn(vb, phys_concat(k_hat).astype(BF16))
    yield

    lane = lax.broadcasted_iota(jnp.int32, (SUB, HEADS * CHUNK), 1)
    col_blk = lax.shift_right_logical(lane, jnp.int32(SUB.bit_length() - 1)) & (NSUB - 1)
    col_p = (NSUB - 1 - col_blk) if rev else col_blk
    bits = [(lane & (1 << b)) != 0 for b in range(SUB.bit_length() - 1)]
    key_pos = lane & (SUB - 1)
    row_pos = lax.broadcasted_iota(jnp.int32, (SUB, HEADS * CHUNK), 0)
    causal = (row_pos <= key_pos) if rev else (row_pos >= key_pos)
    p_rows = []
    for p, I in enumerate(phys):
        if min_step_log2 is None:
            level = [w_all[(I * SUB + jj) * SUB:(I * SUB + jj + 1) * SUB, :] for jj in range(SUB)]
            for bit in bits:
                level = [jnp.where(bit, level[2 * n + 1], level[2 * n]) for n in range(len(level) // 2)]
            diag = level[0]
        else:
            diag = jnp.where(causal, sc_diag[rows(I), :], 0.0)
        acc_p = jnp.where(col_p == p, diag, 0.0)
        for dist in range(1, p + 1):
            n = lhs_at[(p, dist)]
            acc_p = jnp.where(col_p == p - dist, sc[n * SUB:(n + 1) * SUB, :], acc_p)
        p_rows.append(acc_p)
    p_all = phys_concat(p_rows).astype(BF16)
    rv = _iota_div((HEADS * CHUNK, REC_W), 0, CHUNK)
    cv = _iota_div((HEADS * CHUNK, REC_W), 1, REC_DV)
    vbd = jnp.where(rv == cv, jnp.concatenate([vb] * HEADS, axis=0), 0)
    o_intra = _dot(p_all, vbd)
    yield

    store_out(o_intra + o_inter)
    rs = _iota_div((REC_W, dkt), 0, REC_DV)
    cs = _iota_div((REC_W, dkt), 1, dk)
    st_ref[...] = st * jnp.exp2(total[0:1, :]) + jnp.where(rs == cs, upd, 0.0)


def _run_interleaved(chains):
    chains = list(chains)
    while chains:
        alive = []
        for ch in chains:
            try:
                next(ch)
                alive.append(ch)
            except StopIteration:
                pass
        chains = alive


def _store_row(o_ref, i, o):
    o_ref[i] = o


def _hg_prep(z_ref, i, lb_ref, d):
    z = z_ref[i]
    lb = lb_ref[d:d + 1, :]
    q = _silu(z[:, 0:REC_W])
    v = z[:, 3 * REC_W:4 * REC_W]
    zz = z[:, (1 + d) * REC_W:(2 + d) * REC_W]
    f = lb + (1.0 - lb) * jax.nn.sigmoid(zz)
    g = jnp.log(jnp.maximum(f, FORGET_MIN))
    k = (1.0 - lb) * jax.nn.sigmoid(-zz)
    return q, k, g, v


def _log_sigmoid(x):
    return jnp.minimum(x, 0.0) - jnp.log(1.0 + jnp.exp(-jnp.abs(x)))


def _gla_prep(z_ref, i, wg_ref, bg_ref, d):
    z = z_ref[i]
    dkt = HEADS * GLA_DK
    q = z[:, 0:dkt] * (GLA_DK ** -0.5)
    k = z[:, dkt:2 * dkt]
    v = z[:, 2 * dkt:2 * dkt + REC_W]
    za = z[:, GLA_COLS - LANE:GLA_COLS].astype(BF16)
    logits = _dot(za, wg_ref[d])
    yield
    g = _log_sigmoid(logits + bg_ref[d]) * (1.0 / GLA_TAU)
    return q, k, g, v


def _rec_kernel(*refs, mixer, dk, bb):
    if mixer == "hg":
        (zf_ref, zb_ref, lb_ref, aloc_ref, ones_ref, sf0_ref, sb0_ref,
         of_ref, ob_ref, sf1_ref, sb1_ref, stf, stb) = refs
    else:
        (zf_ref, zb_ref, wg_ref, bg_ref, aloc_ref, ones_ref, sf0_ref, sb0_ref,
         of_ref, ob_ref, sf1_ref, sb1_ref, stf, stb) = refs
    c = pl.program_id(1)

    @pl.when(c == 0)
    def _():
        stf[...] = sf0_ref[...]
        stb[...] = sb0_ref[...]

    head_ones = ones_ref[...]
    min_step_log2 = float(np.log2(FORGET_MIN)) if mixer == "hg" else None
    chains = []
    for i in range(bb):
        for d, (z_ref, o_ref, st) in enumerate(((zf_ref, of_ref, stf), (zb_ref, ob_ref, stb))):
            if mixer == "hg":
                prep = functools.partial(_hg_prep, z_ref, i, lb_ref, d)
            else:
                prep = functools.partial(_gla_prep, z_ref, i, wg_ref, bg_ref, d)
            chains.append(_rec_chain(prep, functools.partial(_store_row, o_ref, i), st.at[i], aloc_ref[d],
                                     head_ones, rev=(d == 1), dk=dk, min_step_log2=min_step_log2))
    _run_interleaved(chains)

    @pl.when(c == pl.num_programs(1) - 1)
    def _():
        sf1_ref[...] = stf[...]
        sb1_ref[...] = stb[...]


def _rec_constants(dk):
    a = np.zeros((2, CHUNK, CHUNK), np.float32)
    for i in range(CHUNK):
        for j in range(CHUNK):
            if i // SUB == j // SUB:
                a[0, i, j] = 1.0 if j <= i else 0.0
                a[1, i, j] = 1.0 if j >= i else 0.0
    dkt = HEADS * dk
    ones = np.zeros((dkt, REC_W), np.float32)
    for r in range(dkt):
        ones[r, (r // dk) * REC_DV:(r // dk + 1) * REC_DV] = 1.0
    return jnp.asarray(a, BF16), jnp.asarray(ones, BF16)


def _recurrence(z, mixer, params, s0f, s0b):
    B, T, cols = z.shape
    dk = HG_DK if mixer == "hg" else GLA_DK
    dkt = HEADS * dk
    nc = T // CHUNK
    bb = REC_BB if B % REC_BB == 0 else 1
    a_loc, head_ones = _rec_constants(dk)
    full = lambda shape: pl.BlockSpec(shape, lambda b, c: (0,) * len(shape))
    in_specs = [pl.BlockSpec((bb, CHUNK, cols), lambda b, c: (b, c, 0)),
                pl.BlockSpec((bb, CHUNK, cols), lambda b, c: (b, nc - 1 - c, 0))]
    in_specs += [full(p.shape) for p in params]
    in_specs += [full(a_loc.shape), full(head_ones.shape),
                 pl.BlockSpec((bb, REC_W, dkt), lambda b, c: (b, 0, 0)),
                 pl.BlockSpec((bb, REC_W, dkt), lambda b, c: (b, 0, 0))]
    out_specs = [pl.BlockSpec((bb, CHUNK, REC_W), lambda b, c: (b, c, 0)),
                 pl.BlockSpec((bb, CHUNK, REC_W), lambda b, c: (b, nc - 1 - c, 0)),
                 pl.BlockSpec((bb, REC_W, dkt), lambda b, c: (b, 0, 0)),
                 pl.BlockSpec((bb, REC_W, dkt), lambda b, c: (b, 0, 0))]
    out_shape = [jax.ShapeDtypeStruct((B, T, REC_W), F32)] * 2 + [jax.ShapeDtypeStruct((B, REC_W, dkt), F32)] * 2
    return pl.pallas_call(
        functools.partial(_rec_kernel, mixer=mixer, dk=dk, bb=bb),
        grid=(B // bb, nc),
        in_specs=in_specs,
        out_specs=out_specs,
        out_shape=out_shape,
        scratch_shapes=[pltpu.VMEM((bb, REC_W, dkt), F32), pltpu.VMEM((bb, REC_W, dkt), F32)],
        compiler_params=_cparams(("arbitrary", "arbitrary")),
        name="recurrence_" + mixer,
    )(z, z, *params, a_loc, head_ones, s0f, s0b)


def _rms_norm(x, g):
    return x * lax.rsqrt(jnp.mean(x * x, axis=-1, keepdims=True) + NORM_EPS) * g


def _swap_halves(x):
    n = x.shape[-1]
    lane = lax.broadcasted_iota(jnp.int32, x.shape, x.ndim - 1)
    first_half = (lane & (2 * ROPE_FREQS - 1)) < ROPE_FREQS
    return jnp.where(first_half, pltpu.roll(x, n - ROPE_FREQS, x.ndim - 1),
                     pltpu.roll(x, ROPE_FREQS, x.ndim - 1))


def _mla_up_kernel(z_ref, qg_ref, kvg_ref, wuq_ref, wuk_ref, wvt_ref, cos_ref, sin_ref,
                   q_ref, k_ref, vt_ref, *, rope):
    z = z_ref[0]
    cq = _rms_norm(z[:, 0:MLA_Q_RANK], qg_ref[...]).astype(BF16)
    ckv = _rms_norm(z[:, MLA_Q_RANK:MLA_Q_RANK + MLA_KV_RANK], kvg_ref[...]).astype(BF16)
    kr = z[:, MLA_Q_RANK + MLA_KV_RANK:MLA_Q_RANK + MLA_KV_RANK + LANE]
    qf = _dot(cq, wuq_ref[...]) * (MLA_SCALE * LOG2E)
    kf = _dot(ckv, wuk_ref[...])
    vt = _dot_nt(wvt_ref[...], ckv)
    q_rope = qf[:, HEADS * MLA_NOPE:]
    if rope:
        cos = cos_ref[...]
        sin = sin_ref[...]
        cos_q = jnp.concatenate([cos, cos], axis=-1)
        sin_q = jnp.concatenate([sin, sin], axis=-1)
        q_rope = q_rope * cos_q + _swap_halves(q_rope) * sin_q
        kr = kr * cos + _swap_halves(kr) * sin
    kr = kr[:, 0:MLA_ROPE]
    ones = jnp.ones((V_ROWS - MLA_DV, vt.shape[1]), BF16)
    for h in range(HEADS):
        q_ref[0, h, :, 0:MLA_NOPE] = qf[:, h * MLA_NOPE:(h + 1) * MLA_NOPE].astype(BF16)
        q_ref[0, h, :, MLA_NOPE:MLA_QK] = q_rope[:, h * MLA_ROPE:(h + 1) * MLA_ROPE].astype(BF16)
        k_ref[0, h, :, 0:MLA_NOPE] = kf[:, h * MLA_NOPE:(h + 1) * MLA_NOPE].astype(BF16)
        k_ref[0, h, :, MLA_NOPE:MLA_QK] = kr.astype(BF16)
        vt_ref[0, h, 0:MLA_DV, :] = vt[h * MLA_DV:(h + 1) * MLA_DV, :].astype(BF16)
        vt_ref[0, h, MLA_DV:V_ROWS, :] = ones


def _mla_up(z_mla, q_norm_g, kv_norm_g, wuq_packed, wuk_packed, wvt_packed, cos_t, sin_t, rope, tm):
    B, T, _ = z_mla.shape
    full = lambda shape: pl.BlockSpec(shape, lambda b, i: (0,) * len(shape))
    return pl.pallas_call(
        functools.partial(_mla_up_kernel, rope=rope),
        grid=(B, T // tm),
        in_specs=[pl.BlockSpec((1, tm, MLA_COLS), lambda b, i: (b, i, 0)),
                  full((1, MLA_Q_RANK)), full((1, MLA_KV_RANK)),
                  full(wuq_packed.shape), full(wuk_packed.shape), full(wvt_packed.shape),
                  pl.BlockSpec((tm, LANE), lambda b, i: (i, 0)),
                  pl.BlockSpec((tm, LANE), lambda b, i: (i, 0))],
        out_specs=[pl.BlockSpec((1, HEADS, tm, MLA_QK), lambda b, i: (b, 0, i, 0)),
                   pl.BlockSpec((1, HEADS, tm, MLA_QK), lambda b, i: (b, 0, i, 0)),
                   pl.BlockSpec((1, HEADS, V_ROWS, tm), lambda b, i: (b, 0, 0, i))],
        out_shape=[jax.ShapeDtypeStruct((B, HEADS, T, MLA_QK), BF16),
                   jax.ShapeDtypeStruct((B, HEADS, T, MLA_QK), BF16),
                   jax.ShapeDtypeStruct((B, HEADS, V_ROWS, T), BF16)],
        compiler_params=_cparams(("arbitrary", "arbitrary")),
        name="mla_up_rope" if rope else "mla_up",
    )(z_mla, q_norm_g.reshape(1, -1), kv_norm_g.reshape(1, -1), wuq_packed, wuk_packed, wvt_packed,
      cos_t, sin_t)


def _attn_stage(q_ref, k_ref, vt_ref, o_ref, s_w, m_w, s_r, m_r):
    n_keys = k_ref.shape[2]
    half = n_keys // 2
    parts = [slice(0, half), slice(half, n_keys)]
    q = q_ref[0, 0]
    m = None
    for ks in parts:
        s = _dot_nt(k_ref[0, 0, ks, :], q)
        s_w[ks, :] = s
        mx = jnp.max(s, axis=0, keepdims=True)
        m = mx if m is None else jnp.maximum(m, mx)
    m_w[0:1, :] = m
    m_prev = m_r[0:1, :]
    acc = None
    for ks in parts:
        part = _dot(vt_ref[0, 0, :, ks], jnp.exp2(s_r[ks, :] - m_prev).astype(BF16))
        acc = part if acc is None else acc + part
    o_t = acc[0:MLA_DV, :] * (1.0 / acc[MLA_DV:MLA_DV + 1, :])
    o_ref[0] = o_t.T


def _attn_kernel(q_ref, k_ref, vt_ref, o_ref, s_a, m_a, s_b, m_b):
    t = pl.program_id(0)

    @pl.when(t == 0)
    def _():
        s_b[...] = jnp.zeros_like(s_b)
        m_b[...] = jnp.zeros_like(m_b)

    @pl.when(t % 2 == 0)
    def _():
        _attn_stage(q_ref, k_ref, vt_ref, o_ref, s_a, m_a, s_b, m_b)

    @pl.when(t % 2 == 1)
    def _():
        _attn_stage(q_ref, k_ref, vt_ref, o_ref, s_b, m_b, s_a, m_a)


def _attention(q, k, vt, tq):
    B, H, N, _ = q.shape
    M = k.shape[2]
    assert M % (2 * LANE) == 0
    nq = N // tq
    n_tiles = B * H * nq

    def tile(t):
        return t // (H * nq), (t // nq) % H, t % nq

    def score_tile(t):
        return tile(jnp.minimum(t, n_tiles - 1))

    def finish_tile(t):
        return tile(jnp.maximum(t - 1, 0))

    def q_map(t):
        b, h, i = score_tile(t)
        return b, h, i, 0

    def k_map(t):
        b, h, _ = score_tile(t)
        return b, h, 0, 0

    def vt_map(t):
        b, h, _ = finish_tile(t)
        return b, h, 0, 0

    def o_map(t):
        b, h, i = finish_tile(t)
        return b, i, h

    return pl.pallas_call(
        _attn_kernel,
        grid=(n_tiles + 1,),
        in_specs=[pl.BlockSpec((1, 1, tq, MLA_QK), q_map),
                  pl.BlockSpec((1, 1, M, MLA_QK), k_map),
                  pl.BlockSpec((1, 1, V_ROWS, M), vt_map)],
        out_specs=pl.BlockSpec((1, tq, MLA_DV), o_map),
        out_shape=jax.ShapeDtypeStruct((B, N, H * MLA_DV), F32),
        scratch_shapes=[pltpu.VMEM((M, tq), F32), pltpu.VMEM((SUBLANE, tq), F32),
                        pltpu.VMEM((M, tq), F32), pltpu.VMEM((SUBLANE, tq), F32)],
        compiler_params=_cparams(("arbitrary",)),
        name="attention",
    )(q, k, vt)


def _head_rms(o, ones_bf, g):
    sq = o * o
    hi = sq.astype(BF16)
    lo = (sq - hi.astype(F32)).astype(BF16)
    ms = (_dot(hi, ones_bf) + _dot(lo, ones_bf)) * (1.0 / REC_DV)
    return o * lax.rsqrt(ms + NORM_EPS) * g


def _out_kernel(x_ref, gate_ref, zhg_ref, zmla_ref, zgla_ref, hgf_ref, hgb_ref, att_ref, glf_ref, glb_ref,
                ones_ref, hgn_ref, gln_ref, w_ref, lng_ref, lnb_ref, o_ref):
    ones_bf = ones_ref[...]
    y_hg = _head_rms(hgf_ref[0] + hgb_ref[0], ones_bf, hgn_ref[...]) * _silu(zhg_ref[0])
    y_mla = att_ref[0] * _silu(zmla_ref[0])
    y_gla = _head_rms(glf_ref[0] + glb_ref[0], ones_bf, gln_ref[...]) * _silu(zgla_ref[0])
    proj = (_dot(y_hg.astype(BF16), w_ref[0:REC_W, :])
            + _dot(y_mla.astype(BF16), w_ref[REC_W:REC_W + MLA_W, :])
            + _dot(y_gla.astype(BF16), w_ref[REC_W + MLA_W:, :]))
    r = DEEPNORM_ALPHA * x_ref[0] + gate_ref[0] * proj
    o_ref[0] = _layer_norm(r) * lng_ref[...] + lnb_ref[...]


def _output_block(x, mod, mod_row, z_hg, z_mla, z_gla, hg_f, hg_b, att, gl_f, gl_b,
                  hg_norm_g, gla_norm_g, w_out_bf, ln_g, ln_b, tm):
    B, T, _ = x.shape
    ones = np.zeros((REC_W, REC_W), np.float32)
    for r in range(REC_W):
        ones[r, (r // REC_DV) * REC_DV:(r // REC_DV + 1) * REC_DV] = 1.0
    ones = jnp.asarray(ones, BF16)
    row = lambda w: pl.BlockSpec((1, tm, w), lambda b, i: (b, i, 0))
    full = lambda shape: pl.BlockSpec(shape, lambda b, i: (0,) * len(shape))
    return pl.pallas_call(
        _out_kernel,
        grid=(B, T // tm),
        in_specs=[row(D_MODEL),
                  pl.BlockSpec((1, 1, D_MODEL), lambda b, i: (mod_row(b), 0, 2)),
                  pl.BlockSpec((1, tm, REC_W), lambda b, i: (b, i, HG_COLS // REC_W - 1)),
                  pl.BlockSpec((1, tm, MLA_W), lambda b, i: (b, i, 1)),
                  pl.BlockSpec((1, tm, REC_W), lambda b, i: (b, i, 2)),
                  row(REC_W), row(REC_W), row(MLA_W), row(REC_W), row(REC_W),
                  full((REC_W, REC_W)), full((1, REC_W)), full((1, REC_W)),
                  full((D_MODEL, D_MODEL)), full((1, D_MODEL)), full((1, D_MODEL))],
        out_specs=row(D_MODEL),
        out_shape=jax.ShapeDtypeStruct((B, T, D_MODEL), F32),
        compiler_params=_cparams(("arbitrary", "arbitrary")),
        name="output_block",
    )(x, mod, z_hg, z_mla, z_gla, hg_f, hg_b, att, gl_f, gl_b, ones,
      jnp.tile(hg_norm_g, HEADS).reshape(1, REC_W), jnp.tile(gla_norm_g, HEADS).reshape(1, REC_W),
      w_out_bf, ln_g.reshape(1, D_MODEL), ln_b.reshape(1, D_MODEL))


def _pack_w_in(w):
    o = np.cumsum([0, 256, 256, 256, 256, 256, 256, 128, 64, 512, 128, 128, 256, 16, 16, 256])
    col = lambda i: w[:, int(o[i]):int(o[i + 1])]
    zeros = lambda n: jnp.zeros((w.shape[0], n), w.dtype)
    hg = [col(0), col(1), col(2), col(3), col(4)]
    mla = [col(5), col(6), col(7), zeros(64), col(8)]
    gla = [col(9), col(10), col(11), col(14), col(12), col(13), zeros(96)]
    return jnp.concatenate(hg + mla + gla, axis=1).astype(BF16)


def _pack_wuq(w):
    w4 = w.reshape(MLA_Q_RANK, HEADS, MLA_QK)
    return jnp.concatenate([w4[:, :, :MLA_NOPE].reshape(MLA_Q_RANK, -1),
                            w4[:, :, MLA_NOPE:].reshape(MLA_Q_RANK, -1)], axis=1).astype(BF16)


def _pack_wukv(w):
    w4 = w.reshape(MLA_KV_RANK, HEADS, MLA_NOPE + MLA_DV)
    wk = w4[:, :, :MLA_NOPE].reshape(MLA_KV_RANK, -1).astype(BF16)
    wvt = w4[:, :, MLA_NOPE:].reshape(MLA_KV_RANK, -1).T.astype(BF16)
    return wk, wvt


def _pack_gla_gate(w_a2, b_a):
    dkt = HEADS * GLA_DK
    wg = jnp.zeros((2, LANE, dkt), F32)
    wg = wg.at[0, 0:GLA_GATE_RANK].set(w_a2[0])
    wg = wg.at[1, GLA_GATE_RANK:2 * GLA_GATE_RANK].set(w_a2[1])
    return wg.astype(BF16), b_a.reshape(2, 1, dkt)


def _rope_tables(n):
    rows = n // GRID_W
    pos_r = jnp.repeat(jnp.arange(rows, dtype=F32), GRID_W)
    pos_c = jnp.tile(jnp.arange(GRID_W, dtype=F32), rows)
    inv = 1.0 / (ROPE_BASE ** (jnp.arange(ROPE_FREQS, dtype=F32) / ROPE_FREQS))
    ang = jnp.stack([pos_r, pos_c], axis=-1)[:, :, None] * inv
    cos = jnp.cos(ang)
    sin = jnp.sin(ang)
    cos64 = jnp.concatenate([cos, cos], axis=-1).reshape(n, MLA_ROPE)
    sin64 = jnp.concatenate([-sin, sin], axis=-1).reshape(n, MLA_ROPE)
    return jnp.tile(cos64, (1, LANE // MLA_ROPE)), jnp.tile(sin64, (1, LANE // MLA_ROPE))


def _pick_tile(t, pref):
    while t % pref:
        pref //= 2
    return pref


def kernel(x, c, ctx, c_ctx, w_mod, b_mod, w_in, w_out, ln_g, ln_b, hg_lb_logits, hg_norm_g,
           mla_q_norm_g, mla_kv_norm_g, mla_w_uq, mla_w_ukv, gla_w_a2, gla_b_a, gla_norm_g):
    B, N, _ = x.shape
    Lc = ctx.shape[1]
    cos_t, sin_t = _rope_tables(N)
    cos_c = jnp.zeros((Lc, LANE), F32)

    lb_soft = jax.nn.softmax(hg_lb_logits.astype(F32), axis=0)
    hg_lb = jnp.clip(jnp.cumsum(lb_soft, axis=0) - lb_soft[0:1], 0.0, 1.0)

    n_rows = -(-(B + 1) // 8) * 8
    c_rows = jnp.zeros((n_rows, D_MODEL), F32).at[0:B].set(c).at[B].set(c_ctx)
    mods = _modulation(c_rows, w_mod, b_mod)

    lat_row = lambda b: b
    ctx_row = lambda b: B
    tm_l, tm_c = _pick_tile(N, 256), _pick_tile(Lc, 256)
    zeros_state = lambda dkt: jnp.zeros((B, REC_W, dkt), F32)

    xc = ctx
    for l in range(DEPTH):
        need_ctx = l < DEPTH - 1
        mod = mods[l].reshape(n_rows, 1, 3 * D_MODEL)
        w_in_p = _pack_w_in(w_in[l])
        zl_hg, zl_mla, zl_gla = _in_projection(x, mod, lat_row, w_in_p, tm_l)
        zc_hg, zc_mla, zc_gla = _in_projection(xc, mod, ctx_row, w_in_p, tm_c)

        hg_params = (hg_lb[l],)
        hc_f, hc_b, s_f, s_b = _recurrence(zc_hg, "hg", hg_params, zeros_state(HEADS * HG_DK),
                                           zeros_state(HEADS * HG_DK))
        hl_f, hl_b, _, _ = _recurrence(zl_hg, "hg", hg_params, s_f, s_b)
        gla_params = _pack_gla_gate(gla_w_a2[l], gla_b_a[l])
        gc_f, gc_b, s_f, s_b = _recurrence(zc_gla, "gla", gla_params, zeros_state(HEADS * GLA_DK),
                                           zeros_state(HEADS * GLA_DK))
        gl_f, gl_b, _, _ = _recurrence(zl_gla, "gla", gla_params, s_f, s_b)

        wuq_p = _pack_wuq(mla_w_uq[l])
        wuk_p, wvt_p = _pack_wukv(mla_w_ukv[l])
        q_l, k_l, vt_l = _mla_up(zl_mla, mla_q_norm_g[l], mla_kv_norm_g[l], wuq_p, wuk_p, wvt_p,
                                 cos_t, sin_t, True, tm_l)
        q_c, k_c, vt_c = _mla_up(zc_mla, mla_q_norm_g[l], mla_kv_norm_g[l], wuq_p, wuk_p, wvt_p,
                                 cos_c, cos_c, False, tm_c)
        att_l = _attention(q_l, jnp.concatenate([k_c, k_l], axis=2),
                           jnp.concatenate([vt_c, vt_l], axis=3), _pick_tile(N, 256))

        w_out_bf = w_out[l].astype(BF16)
        x_new = _output_block(x, mod, lat_row, zl_hg, zl_mla, zl_gla, hl_f, hl_b, att_l, gl_f, gl_b,
                              hg_norm_g[l], gla_norm_g[l], w_out_bf, ln_g[l], ln_b[l], tm_l)
        if need_ctx:
            att_c = _attention(q_c, k_c, vt_c, _pick_tile(Lc, 256))
            xc = _output_block(xc, mod, ctx_row, zc_hg, zc_mla, zc_gla, hc_f, hc_b, att_c, gc_f, gc_b,
                               hg_norm_g[l], gla_norm_g[l], w_out_bf, ln_g[l], ln_b[l], tm_c)
        x = x_new
    return x
```

```python
import functools
import inspect

import numpy as np
import jax
import jax.numpy as jnp
from jax import lax
from jax.experimental import pallas as pl
from jax.experimental.pallas import tpu as pltpu

F32 = jnp.float32
BF16 = jnp.bfloat16
HIGHEST = lax.Precision.HIGHEST

D_MODEL = 1024
DEPTH = 2
GRID_W = 64
HEADS = 4
HG_DK = 64
GLA_DK = 32
REC_DV = 64
REC_W = HEADS * REC_DV
MLA_NOPE = 128
MLA_ROPE = 64
MLA_DV = 128
MLA_Q_RANK = 256
MLA_KV_RANK = 128
MLA_QK = MLA_NOPE + MLA_ROPE
MLA_W = HEADS * MLA_DV
MLA_SCALE = MLA_QK ** -0.5
GLA_GATE_RANK = 16
GLA_TAU = 16.0
ROPE_BASE = 10000.0
ROPE_FREQS = MLA_ROPE // 4
NORM_EPS = 1e-6
FORGET_MIN = 1e-6
DEEPNORM_ALPHA = (2 * DEPTH) ** 0.25

CHUNK = 64
LANE = 128
SUBLANE = 8
SUB = SUBLANE
NSUB = CHUNK // SUB
REC_BB = 8
ROW_TILE = 256
LOG2E = 1.4426950408889634
V_ROWS = MLA_DV + 16
HG_COLS = 5 * REC_W
MLA_COLS = 1024
GLA_COLS = 896
IN_COLS = HG_COLS + MLA_COLS + GLA_COLS
MLA_GATE_COL = MLA_COLS - MLA_W

VMEM_LIMIT = 56 * 1024 * 1024


def _cparams(sem):
    return pltpu.CompilerParams(dimension_semantics=sem, vmem_limit_bytes=VMEM_LIMIT)


def _dot(a, b):
    return jnp.dot(a, b, preferred_element_type=F32)


def _dot_nt(a, b):
    return lax.dot_general(a, b, (((1,), (1,)), ((), ())), preferred_element_type=F32)


def _dot_tn(a, b):
    return lax.dot_general(a, b, (((0,), (0,)), ((), ())), preferred_element_type=F32)


def _silu(x):
    return x * jax.nn.sigmoid(x)


def _iota_div(shape, axis, div):
    assert div & (div - 1) == 0
    return lax.shift_right_logical(lax.broadcasted_iota(jnp.int32, shape, axis),
                                   jnp.int32(div.bit_length() - 1))


def _mod_kernel(c_ref, w_ref, b_ref, o_ref):
    s = _silu(c_ref[...])
    o_ref[0] = jnp.dot(s, w_ref[0], precision=HIGHEST, preferred_element_type=F32) + b_ref[0]


def _modulation(c_rows, w_mod, b_mod):
    rows = c_rows.shape[0]
    nt = 3 * D_MODEL // 1024
    return pl.pallas_call(
        _mod_kernel,
        grid=(DEPTH, nt),
        in_specs=[pl.BlockSpec((rows, D_MODEL), lambda l, j: (0, 0)),
                  pl.BlockSpec((1, D_MODEL, 1024), lambda l, j: (l, 0, j)),
                  pl.BlockSpec((1, 1, 1024), lambda l, j: (l, 0, j))],
        out_specs=pl.BlockSpec((1, rows, 1024), lambda l, j: (l, 0, j)),
        out_shape=jax.ShapeDtypeStruct((DEPTH, rows, 3 * D_MODEL), F32),
        compiler_params=_cparams(("arbitrary", "arbitrary")),
        name="modulation",
    )(c_rows, w_mod, b_mod.reshape(DEPTH, 1, 3 * D_MODEL))


def _layer_norm(x):
    mu = jnp.mean(x, axis=-1, keepdims=True)
    xc = x - mu
    var = jnp.mean(xc * xc, axis=-1, keepdims=True)
    return xc * lax.rsqrt(var + NORM_EPS)


def _rms_norm(x, g):
    return x * lax.rsqrt(jnp.mean(x * x, axis=-1, keepdims=True) + NORM_EPS) * g


def _swap_halves(x):
    n = x.shape[-1]
    lane = lax.broadcasted_iota(jnp.int32, x.shape, x.ndim - 1)
    first_half = (lane & (2 * ROPE_FREQS - 1)) < ROPE_FREQS
    return jnp.where(first_half, pltpu.roll(x, n - ROPE_FREQS, x.ndim - 1),
                     pltpu.roll(x, ROPE_FREQS, x.ndim - 1))


def _front_kernel(x_ref, xc_ref, shift_ref, scale_ref, w_ref, qg_ref, kvg_ref, wuq_ref, wuk_ref, wvt_ref,
                  cos_ref, sin_ref, zhg_ref, zgla_ref, mgate_ref, q_ref, k_ref, vt_ref, *, n_ctx_tiles):
    is_ctx = pl.program_id(1) < n_ctx_tiles
    x = jnp.where(is_ctx, xc_ref[0], x_ref[0])
    hb = (_layer_norm(x) * (1.0 + scale_ref[0]) + shift_ref[0]).astype(BF16)
    zhg_ref[0] = _dot(hb, w_ref[:, 0:HG_COLS]).astype(BF16)
    zgla_ref[0] = _dot(hb, w_ref[:, HG_COLS + MLA_COLS:IN_COLS]).astype(BF16)
    z = _dot(hb, w_ref[:, HG_COLS:HG_COLS + MLA_COLS])
    mgate_ref[0] = z[:, MLA_GATE_COL:].astype(BF16)

    cq = _rms_norm(z[:, 0:MLA_Q_RANK], qg_ref[...]).astype(BF16)
    ckv = _rms_norm(z[:, MLA_Q_RANK:MLA_Q_RANK + MLA_KV_RANK], kvg_ref[...]).astype(BF16)
    kr = z[:, MLA_Q_RANK + MLA_KV_RANK:MLA_Q_RANK + MLA_KV_RANK + LANE]
    qf = _dot(cq, wuq_ref[...]) * (MLA_SCALE * LOG2E)
    kf = _dot(ckv, wuk_ref[...])
    vt = _dot_nt(wvt_ref[...], ckv)
    cos = cos_ref[...]
    sin = sin_ref[...]
    q_rope = qf[:, HEADS * MLA_NOPE:]
    q_rope = (q_rope * jnp.concatenate([cos, cos], axis=-1)
              + _swap_halves(q_rope) * jnp.concatenate([sin, sin], axis=-1))
    kr = (kr * cos + _swap_halves(kr) * sin)[:, 0:MLA_ROPE]
    ones = jnp.ones((V_ROWS - MLA_DV, vt.shape[1]), BF16)
    for h in range(HEADS):
        q_ref[0, h, :, 0:MLA_NOPE] = qf[:, h * MLA_NOPE:(h + 1) * MLA_NOPE].astype(BF16)
        q_ref[0, h, :, MLA_NOPE:MLA_QK] = q_rope[:, h * MLA_ROPE:(h + 1) * MLA_ROPE].astype(BF16)
        k_ref[0, h, :, 0:MLA_NOPE] = kf[:, h * MLA_NOPE:(h + 1) * MLA_NOPE].astype(BF16)
        k_ref[0, h, :, MLA_NOPE:MLA_QK] = kr.astype(BF16)
        vt_ref[0, h, 0:MLA_DV, :] = vt[h * MLA_DV:(h + 1) * MLA_DV, :].astype(BF16)
        vt_ref[0, h, MLA_DV:V_ROWS, :] = ones


def _front(x, xc, mod, w_in_packed, q_norm_g, kv_norm_g, wuq_packed, wuk_packed, wvt_packed, cos_t, sin_t):
    B, N, _ = x.shape
    Lc = xc.shape[1]
    T = Lc + N
    tm = ROW_TILE
    nct = Lc // tm
    full = lambda shape: pl.BlockSpec(shape, lambda b, i: (0,) * len(shape))
    mod_row = lambda b, i: jnp.where(i < nct, B, b)
    return pl.pallas_call(
        functools.partial(_front_kernel, n_ctx_tiles=nct),
        grid=(B, T // tm),
        in_specs=[pl.BlockSpec((1, tm, D_MODEL), lambda b, i: (b, jnp.maximum(i - nct, 0), 0)),
                  pl.BlockSpec((1, tm, D_MODEL), lambda b, i: (b, jnp.minimum(i, nct - 1), 0)),
                  pl.BlockSpec((1, 1, D_MODEL), lambda b, i: (mod_row(b, i), 0, 0)),
                  pl.BlockSpec((1, 1, D_MODEL), lambda b, i: (mod_row(b, i), 0, 1)),
                  full((D_MODEL, IN_COLS)), full((1, MLA_Q_RANK)), full((1, MLA_KV_RANK)),
                  full(wuq_packed.shape), full(wuk_packed.shape), full(wvt_packed.shape),
                  pl.BlockSpec((tm, LANE), lambda b, i: (i, 0)),
                  pl.BlockSpec((tm, LANE), lambda b, i: (i, 0))],
        out_specs=[pl.BlockSpec((1, tm, HG_COLS), lambda b, i: (b, i, 0)),
                   pl.BlockSpec((1, tm, GLA_COLS), lambda b, i: (b, i, 0)),
                   pl.BlockSpec((1, tm, MLA_W), lambda b, i: (b, i, 0)),
                   pl.BlockSpec((1, HEADS, tm, MLA_QK), lambda b, i: (b, 0, i, 0)),
                   pl.BlockSpec((1, HEADS, tm, MLA_QK), lambda b, i: (b, 0, i, 0)),
                   pl.BlockSpec((1, HEADS, V_ROWS, tm), lambda b, i: (b, 0, 0, i))],
        out_shape=[jax.ShapeDtypeStruct((B, T, HG_COLS), BF16),
                   jax.ShapeDtypeStruct((B, T, GLA_COLS), BF16),
                   jax.ShapeDtypeStruct((B, T, MLA_W), BF16),
                   jax.ShapeDtypeStruct((B, HEADS, T, MLA_QK), BF16),
                   jax.ShapeDtypeStruct((B, HEADS, T, MLA_QK), BF16),
                   jax.ShapeDtypeStruct((B, HEADS, V_ROWS, T), BF16)],
        compiler_params=_cparams(("arbitrary", "arbitrary")),
        name="front",
    )(x, xc, mod, mod, w_in_packed, q_norm_g.reshape(1, -1), kv_norm_g.reshape(1, -1),
      wuq_packed, wuk_packed, wvt_packed, cos_t, sin_t)


def _split3(g):
    hi = g.astype(BF16)
    r1 = g - hi.astype(F32)
    mid = r1.astype(BF16)
    lo = (r1 - mid.astype(F32)).astype(BF16)
    return hi, mid, lo


def _rec_chain(prep, store_out, st_ref, a_loc, head_ones, rev, dk, min_step_log2):
    dkt = HEADS * dk
    inputs = prep()
    q, k, g, v = (yield from inputs) if inspect.isgenerator(inputs) else inputs
    ghi, gmid, glo = _split3(g * LOG2E)
    lcum = _dot(a_loc, ghi) + _dot(a_loc, gmid) + _dot(a_loc, glo)
    yield

    phys = [(NSUB - 1 - p) if rev else p for p in range(NSUB)]
    rows = lambda I: slice(I * SUB, (I + 1) * SUB)
    last_row = [(I * SUB) if rev else (I * SUB + SUB - 1) for I in phys]
    ltot = [jnp.broadcast_to(lcum[r:r + 1, :], (SUB, dkt)) for r in last_row]

    q_in, k_out = [], []
    for p, I in enumerate(phys):
        lc = lcum[rows(I), :]
        q_in.append(q[rows(I), :] * jnp.exp2(lc))
        k_out.append(k[rows(I), :] * jnp.exp2(ltot[p] - lc))

    def running(order):
        sums, acc = {}, None
        for p in order:
            sums[p] = acc
            acc = ltot[p] if acc is None else acc + ltot[p]
        return sums, acc

    pre, total = running(range(NSUB))
    suf, _ = running(reversed(range(NSUB)))
    scaled = lambda x, e: x if e is None else x * jnp.exp2(e)
    q_hat = [scaled(q_in[p], pre[p]) for p in range(NSUB)]
    k_hat = [scaled(k_out[p], suf[p]) for p in range(NSUB)]

    def phys_concat(parts):
        order = sorted(range(NSUB), key=lambda p: phys[p])
        return jnp.concatenate([parts[p] for p in order], axis=0)

    lhs, lhs_at = [], {}
    for p in range(1, NSUB):
        gap = None
        for dist in range(1, p + 1):
            lhs_at[(p, dist)] = len(lhs)
            lhs.append(scaled(q_in[p], gap))
            gap = ltot[p - dist] if gap is None else gap + ltot[p - dist]
    k_out_all = phys_concat(k_out).astype(BF16)
    rk = _iota_div((HEADS * CHUNK, dkt), 0, CHUNK)
    ck = _iota_div((HEADS * CHUNK, dkt), 1, dk)
    kbd_t = jnp.where(rk == ck, jnp.concatenate([k_out_all] * HEADS, axis=0), 0)
    sc = _dot_nt(jnp.concatenate(lhs, axis=0).astype(BF16), kbd_t)

    if min_step_log2 is None:
        ri = lax.broadcasted_iota(jnp.int32, (SUB, dkt), 0)
        units = []
        for I in range(NSUB):
            q_i = q[rows(I), :]
            l_i = lcum[rows(I), :]
            for jj in range(SUB):
                r = I * SUB + jj
                diff = l_i - lcum[r:r + 1, :]
                if jj != (SUB - 1 if rev else 0):
                    diff = jnp.where((ri <= jj) if rev else (ri >= jj), diff, -1e30)
                units.append(q_i * k[r:r + 1, :] * jnp.exp2(diff))
        w_all = _dot(jnp.concatenate(units, axis=0).astype(BF16), head_ones)
    else:
        assert (SUB // 2) * -min_step_log2 < 120
        q_mid, k_mid = [], []
        for p, I in enumerate(phys):
            m = I * SUB + (SUB // 2 if rev else SUB // 2 - 1)
            rel = lcum[rows(I), :] - jnp.broadcast_to(lcum[m:m + 1, :], (SUB, dkt))
            q_mid.append(q[rows(I), :] * jnp.exp2(rel))
            k_mid.append(k[rows(I), :] * jnp.exp2(-rel))
        k_mid_all = phys_concat(k_mid).astype(BF16)
        kbd_mid_t = jnp.where(rk == ck, jnp.concatenate([k_mid_all] * HEADS, axis=0), 0)
        sc_diag = _dot_nt(phys_concat(q_mid).astype(BF16), kbd_mid_t)

    st = st_ref[...]
    vb = v.astype(BF16)
    o_inter = _dot_nt(phys_concat(q_hat).astype(BF16), st.astype(BF16))
    upd = _dot_tn(vb, phys_concat(k_hat).astype(BF16))
    yield

    lane = lax.broadcasted_iota(jnp.int32, (SUB, HEADS * CHUNK), 1)
    col_blk = lax.shift_right_logical(lane, jnp.int32(SUB.bit_length() - 1)) & (NSUB - 1)
    col_p = (NSUB - 1 - col_blk) if rev else col_blk
    bits = [(lane & (1 << b)) != 0 for b in range(SUB.bit_length() - 1)]
    key_pos = lane & (SUB - 1)
    row_pos = lax.broadcasted_iota(jnp.int32, (SUB, HEADS * CHUNK), 0)
    causal = (row_pos <= key_pos) if rev else (row_pos >= key_pos)
    p_rows = []
    for p, I in enumerate(phys):
        if min_step_log2 is None:
            level = [w_all[(I * SUB + jj) * SUB:(I * SUB + jj + 1) * SUB, :] for jj in range(SUB)]
            for bit in bits:
                level = [jnp.where(bit, level[2 * n + 1], level[2 * n]) for n in range(len(level) // 2)]
            diag = level[0]
        else:
            diag = jnp.where(causal, sc_diag[rows(I), :], 0.0)
        acc_p = jnp.where(col_p == p, diag, 0.0)
        for dist in range(1, p + 1):
            n = lhs_at[(p, dist)]
            acc_p = jnp.where(col_p == p - dist, sc[n * SUB:(n + 1) * SUB, :], acc_p)
        p_rows.append(acc_p)
    p_all = phys_concat(p_rows).astype(BF16)
    rv = _iota_div((HEADS * CHUNK, REC_W), 0, CHUNK)
    cv = _iota_div((HEADS * CHUNK, REC_W), 1, REC_DV)
    vbd = jnp.where(rv == cv, jnp.concatenate([vb] * HEADS, axis=0), 0)
    o_intra = _dot(p_all, vbd)
    yield

    store_out(o_intra + o_inter)
    rs = _iota_div((REC_W, dkt), 0, REC_DV)
    cs = _iota_div((REC_W, dkt), 1, dk)
    st_ref[...] = st * jnp.exp2(total[0:1, :]) + jnp.where(rs == cs, upd, 0.0)


def _run_interleaved(chains):
    chains = list(chains)
    while chains:
        alive = []
        for ch in chains:
            try:
                next(ch)
                alive.append(ch)
            except StopIteration:
                pass
        chains = alive


def _store_row(o_ref, i, o):
    o_ref[i] = o.astype(o_ref.dtype)


def _hg_prep(z_ref, i, lb_ref, d):
    col = lambda n: z_ref[i, :, n * REC_W:(n + 1) * REC_W].astype(F32)
    lb = lb_ref[d:d + 1, :]
    q = _silu(col(0))
    v = col(3)
    zz = col(1 + d)
    f = lb + (1.0 - lb) * jax.nn.sigmoid(zz)
    g = jnp.log(jnp.maximum(f, FORGET_MIN))
    k = (1.0 - lb) * jax.nn.sigmoid(-zz)
    return q, k, g, v


def _log_sigmoid(x):
    return jnp.minimum(x, 0.0) - jnp.log(1.0 + jnp.exp(-jnp.abs(x)))


def _gla_prep(z_ref, i, wg_ref, bg_ref, d):
    dkt = HEADS * GLA_DK
    q = z_ref[i, :, 0:dkt].astype(F32) * (GLA_DK ** -0.5)
    k = z_ref[i, :, dkt:2 * dkt].astype(F32)
    v = z_ref[i, :, 2 * dkt:2 * dkt + REC_W].astype(F32)
    logits = _dot(z_ref[i, :, GLA_COLS - LANE:GLA_COLS], wg_ref[d])
    yield
    g = _log_sigmoid(logits + bg_ref[d]) * (1.0 / GLA_TAU)
    return q, k, g, v


def _rec_kernel(*refs, mixer, dk, bb):
    if mixer == "hg":
        zf_ref, zb_ref, lb_ref, aloc_ref, ones_ref, of_ref, ob_ref, stf, stb = refs
    else:
        zf_ref, zb_ref, wg_ref, bg_ref, aloc_ref, ones_ref, of_ref, ob_ref, stf, stb = refs

    @pl.when(pl.program_id(1) == 0)
    def _():
        stf[...] = jnp.zeros_like(stf)
        stb[...] = jnp.zeros_like(stb)

    head_ones = ones_ref[...]
    min_step_log2 = float(np.log2(FORGET_MIN)) if mixer == "hg" else None
    chains = []
    for i in range(bb):
        for d, (z_ref, o_ref, st) in enumerate(((zf_ref, of_ref, stf), (zb_ref, ob_ref, stb))):
            if mixer == "hg":
                prep = functools.partial(_hg_prep, z_ref, i, lb_ref, d)
            else:
                prep = functools.partial(_gla_prep, z_ref, i, wg_ref, bg_ref, d)
            chains.append(_rec_chain(prep, functools.partial(_store_row, o_ref, i), st.at[i], aloc_ref[d],
                                     head_ones, rev=(d == 1), dk=dk, min_step_log2=min_step_log2))
    _run_interleaved(chains)


def _rec_constants(dk):
    a = np.zeros((2, CHUNK, CHUNK), np.float32)
    for i in range(CHUNK):
        for j in range(CHUNK):
            if i // SUB == j // SUB:
                a[0, i, j] = 1.0 if j <= i else 0.0
                a[1, i, j] = 1.0 if j >= i else 0.0
    dkt = HEADS * dk
    ones = np.zeros((dkt, REC_W), np.float32)
    for r in range(dkt):
        ones[r, (r // dk) * REC_DV:(r // dk + 1) * REC_DV] = 1.0
    return jnp.asarray(a, BF16), jnp.asarray(ones, BF16)


def _recurrence(z, mixer, params, n_ctx):
    B, T, cols = z.shape
    dk = HG_DK if mixer == "hg" else GLA_DK
    dkt = HEADS * dk
    nc, ncc = T // CHUNK, n_ctx // CHUNK
    bb = REC_BB if B % REC_BB == 0 else 1
    a_loc, head_ones = _rec_constants(dk)
    full = lambda shape: pl.BlockSpec(shape, lambda b, c: (0,) * len(shape))
    fwd = lambda b, c: (b, c, 0)
    bwd = lambda b, c: (b, jnp.where(c < ncc, ncc - 1 - c, nc + ncc - 1 - c), 0)
    in_specs = [pl.BlockSpec((bb, CHUNK, cols), fwd), pl.BlockSpec((bb, CHUNK, cols), bwd)]
    in_specs += [full(p.shape) for p in params]
    in_specs += [full(a_loc.shape), full(head_ones.shape)]
    return pl.pallas_call(
        functools.partial(_rec_kernel, mixer=mixer, dk=dk, bb=bb),
        grid=(B // bb, nc),
        in_specs=in_specs,
        out_specs=[pl.BlockSpec((bb, CHUNK, REC_W), fwd), pl.BlockSpec((bb, CHUNK, REC_W), bwd)],
        out_shape=[jax.ShapeDtypeStruct((B, T, REC_W), BF16)] * 2,
        scratch_shapes=[pltpu.VMEM((bb, REC_W, dkt), F32), pltpu.VMEM((bb, REC_W, dkt), F32)],
        compiler_params=_cparams(("arbitrary", "arbitrary")),
        name="recurrence_" + mixer,
    )(z, z, *params, a_loc, head_ones)


def _attn_stage(q_ref, k_ref, vt_ref, o_ref, s_w, m_w, s_r, m_r):
    n_keys = k_ref.shape[2]
    half = n_keys // 2
    parts = [slice(0, half), slice(half, n_keys)]
    q = q_ref[0, 0]
    m = None
    for ks in parts:
        s = _dot_nt(k_ref[0, 0, ks, :], q)
        s_w[ks, :] = s
        mx = jnp.max(s, axis=0, keepdims=True)
        m = mx if m is None else jnp.maximum(m, mx)
    m_w[0:1, :] = m
    m_prev = m_r[0:1, :]
    acc = None
    for ks in parts:
        part = _dot(vt_ref[0, 0, :, ks], jnp.exp2(s_r[ks, :] - m_prev).astype(BF16))
        acc = part if acc is None else acc + part
    o_t = acc[0:MLA_DV, :] * (1.0 / acc[MLA_DV:MLA_DV + 1, :])
    o_ref[0] = o_t.T.astype(o_ref.dtype)


def _attn_kernel(q_ref, k_ref, vt_ref, o_ref, s_a, m_a, s_b, m_b):
    t = pl.program_id(0)

    @pl.when(t == 0)
    def _():
        s_b[...] = jnp.zeros_like(s_b)
        m_b[...] = jnp.zeros_like(m_b)

    @pl.when(t % 2 == 0)
    def _():
        _attn_stage(q_ref, k_ref, vt_ref, o_ref, s_a, m_a, s_b, m_b)

    @pl.when(t % 2 == 1)
    def _():
        _attn_stage(q_ref, k_ref, vt_ref, o_ref, s_b, m_b, s_a, m_a)


def _attention(q, k, vt, row0, n_rows, n_keys):
    B, H, _, _ = q.shape
    tq = ROW_TILE
    assert n_keys % (2 * LANE) == 0 and row0 % tq == 0 and n_rows % tq == 0
    nq, q0 = n_rows // tq, row0 // tq
    n_tiles = B * H * nq

    def tile(t):
        return t // (H * nq), (t // nq) % H, t % nq

    def score_tile(t):
        return tile(jnp.minimum(t, n_tiles - 1))

    def finish_tile(t):
        return tile(jnp.maximum(t - 1, 0))

    def q_map(t):
        b, h, i = score_tile(t)
        return b, h, q0 + i, 0

    def k_map(t):
        b, h, _ = score_tile(t)
        return b, h, 0, 0

    def vt_map(t):
        b, h, _ = finish_tile(t)
        return b, h, 0, 0

    def o_map(t):
        b, h, i = finish_tile(t)
        return b, i, h

    return pl.pallas_call(
        _attn_kernel,
        grid=(n_tiles + 1,),
        in_specs=[pl.BlockSpec((1, 1, tq, MLA_QK), q_map),
                  pl.BlockSpec((1, 1, n_keys, MLA_QK), k_map),
                  pl.BlockSpec((1, 1, V_ROWS, n_keys), vt_map)],
        out_specs=pl.BlockSpec((1, tq, MLA_DV), o_map),
        out_shape=jax.ShapeDtypeStruct((B, n_rows, H * MLA_DV), BF16),
        scratch_shapes=[pltpu.VMEM((n_keys, tq), F32), pltpu.VMEM((SUBLANE, tq), F32),
                        pltpu.VMEM((n_keys, tq), F32), pltpu.VMEM((SUBLANE, tq), F32)],
        compiler_params=_cparams(("arbitrary",)),
        name="attention",
    )(q, k, vt)


def _head_rms(o, ones_bf, g):
    sq = o * o
    hi = sq.astype(BF16)
    lo = (sq - hi.astype(F32)).astype(BF16)
    ms = (_dot(hi, ones_bf) + _dot(lo, ones_bf)) * (1.0 / REC_DV)
    return o * lax.rsqrt(ms + NORM_EPS) * g


def _out_kernel(*refs, n_ctx_tiles):
    if n_ctx_tiles:
        x_ref, att_ref, xc_ref, attc_ref = refs[0:4]
        rest = refs[4:]
    else:
        x_ref, att_ref = refs[0:2]
        rest = refs[2:]
    (gate_ref, zhg_ref, zmla_ref, zgla_ref, hgf_ref, hgb_ref, glf_ref, glb_ref,
     ones_ref, hgn_ref, gln_ref, w_ref, lng_ref, lnb_ref) = rest[0:14]
    x, att = x_ref[0], att_ref[0]
    if n_ctx_tiles:
        is_ctx = pl.program_id(1) < n_ctx_tiles
        x = jnp.where(is_ctx, xc_ref[0], x)
        att = jnp.where(is_ctx, attc_ref[0], att)
    f32 = lambda ref: ref[0].astype(F32)
    ones_bf = ones_ref[...]
    y_hg = _head_rms(f32(hgf_ref) + f32(hgb_ref), ones_bf, hgn_ref[...]) * _silu(f32(zhg_ref))
    y_mla = att.astype(F32) * _silu(f32(zmla_ref))
    y_gla = _head_rms(f32(glf_ref) + f32(glb_ref), ones_bf, gln_ref[...]) * _silu(f32(zgla_ref))
    proj = (_dot(y_hg.astype(BF16), w_ref[0:REC_W, :])
            + _dot(y_mla.astype(BF16), w_ref[REC_W:REC_W + MLA_W, :])
            + _dot(y_gla.astype(BF16), w_ref[REC_W + MLA_W:, :]))
    r = DEEPNORM_ALPHA * x + gate_ref[0] * proj
    res = _layer_norm(r) * lng_ref[...] + lnb_ref[...]
    o_ref = rest[14]
    o_ref[0] = res
    if n_ctx_tiles:
        oc_ref = rest[15]

        @pl.when(is_ctx)
        def _():
            oc_ref[0] = res


def _output_block(x, att, xc, att_c, mod, z_hg, mla_gate, z_gla, hg_f, hg_b, gl_f, gl_b,
                  hg_norm_g, gla_norm_g, w_out_bf, ln_g, ln_b):
    B, N, _ = x.shape
    T = z_hg.shape[1]
    tm = ROW_TILE
    with_ctx = xc is not None
    nct = (T - N) // tm
    first = 0 if with_ctx else nct
    lat = lambda i: jnp.maximum(i + first - nct, 0)
    ones = np.zeros((REC_W, REC_W), np.float32)
    for r in range(REC_W):
        ones[r, (r // REC_DV) * REC_DV:(r // REC_DV + 1) * REC_DV] = 1.0
    ones = jnp.asarray(ones, BF16)
    row = lambda w, col=0: pl.BlockSpec((1, tm, w), lambda b, i: (b, i + first, col))
    lat_row = lambda w: pl.BlockSpec((1, tm, w), lambda b, i: (b, lat(i), 0))
    ctx_row = lambda w: pl.BlockSpec((1, tm, w), lambda b, i: (b, jnp.minimum(i, nct - 1), 0))
    full = lambda shape: pl.BlockSpec(shape, lambda b, i: (0,) * len(shape))
    mod_row = (lambda b, i: jnp.where(i < nct, B, b)) if with_ctx else (lambda b, i: b)
    in_specs = [lat_row(D_MODEL), lat_row(MLA_W)]
    args = [x, att]
    if with_ctx:
        in_specs += [ctx_row(D_MODEL), ctx_row(MLA_W)]
        args += [xc, att_c]
    in_specs += [pl.BlockSpec((1, 1, D_MODEL), lambda b, i: (mod_row(b, i), 0, 2)),
                 row(REC_W, HG_COLS // REC_W - 1),
                 row(MLA_W),
                 row(REC_W, 2),
                 row(REC_W), row(REC_W), row(REC_W), row(REC_W),
                 full((REC_W, REC_W)), full((1, REC_W)), full((1, REC_W)),
                 full((D_MODEL, D_MODEL)), full((1, D_MODEL)), full((1, D_MODEL))]
    args += [mod, z_hg, mla_gate, z_gla, hg_f, hg_b, gl_f, gl_b, ones,
             jnp.tile(hg_norm_g, HEADS).reshape(1, REC_W), jnp.tile(gla_norm_g, HEADS).reshape(1, REC_W),
             w_out_bf, ln_g.reshape(1, D_MODEL), ln_b.reshape(1, D_MODEL)]
    out_specs = [lat_row(D_MODEL)]
    out_shape = [jax.ShapeDtypeStruct((B, N, D_MODEL), F32)]
    if with_ctx:
        out_specs.append(ctx_row(D_MODEL))
        out_shape.append(jax.ShapeDtypeStruct(xc.shape, F32))
    outs = pl.pallas_call(
        functools.partial(_out_kernel, n_ctx_tiles=nct if with_ctx else 0),
        grid=(B, T // tm - first),
        in_specs=in_specs,
        out_specs=out_specs,
        out_shape=out_shape,
        compiler_params=_cparams(("arbitrary", "arbitrary")),
        name="output_block",
    )(*args)
    return (outs[0], outs[1]) if with_ctx else (outs[0], None)


def _pack_w_in(w):
    o = np.cumsum([0, 256, 256, 256, 256, 256, 256, 128, 64, 512, 128, 128, 256, 16, 16, 256])
    col = lambda i: w[:, int(o[i]):int(o[i + 1])]
    zeros = lambda n: jnp.zeros((w.shape[0], n), w.dtype)
    hg = [col(0), col(1), col(2), col(3), col(4)]
    mla = [col(5), col(6), col(7), zeros(64), col(8)]
    gla = [col(9), col(10), col(11), col(14), col(12), col(13), zeros(96)]
    return jnp.concatenate(hg + mla + gla, axis=1).astype(BF16)


def _pack_wuq(w):
    w4 = w.reshape(MLA_Q_RANK, HEADS, MLA_QK)
    return jnp.concatenate([w4[:, :, :MLA_NOPE].reshape(MLA_Q_RANK, -1),
                            w4[:, :, MLA_NOPE:].reshape(MLA_Q_RANK, -1)], axis=1).astype(BF16)


def _pack_wukv(w):
    w4 = w.reshape(MLA_KV_RANK, HEADS, MLA_NOPE + MLA_DV)
    wk = w4[:, :, :MLA_NOPE].reshape(MLA_KV_RANK, -1).astype(BF16)
    wvt = w4[:, :, MLA_NOPE:].reshape(MLA_KV_RANK, -1).T.astype(BF16)
    return wk, wvt


def _pack_gla_gate(w_a2, b_a):
    dkt = HEADS * GLA_DK
    wg = jnp.zeros((2, LANE, dkt), F32)
    wg = wg.at[0, 0:GLA_GATE_RANK].set(w_a2[0])
    wg = wg.at[1, GLA_GATE_RANK:2 * GLA_GATE_RANK].set(w_a2[1])
    return wg.astype(BF16), b_a.reshape(2, 1, dkt)


def _rope_tables(n_ctx, n):
    rows = n // GRID_W
    pos_r = jnp.repeat(jnp.arange(rows, dtype=F32), GRID_W)
    pos_c = jnp.tile(jnp.arange(GRID_W, dtype=F32), rows)
    inv = 1.0 / (ROPE_BASE ** (jnp.arange(ROPE_FREQS, dtype=F32) / ROPE_FREQS))
    ang = jnp.stack([pos_r, pos_c], axis=-1)[:, :, None] * inv
    cos = jnp.cos(ang)
    sin = jnp.sin(ang)
    cos64 = jnp.concatenate([cos, cos], axis=-1).reshape(n, MLA_ROPE)
    sin64 = jnp.concatenate([-sin, sin], axis=-1).reshape(n, MLA_ROPE)
    cos_t = jnp.tile(cos64, (1, LANE // MLA_ROPE))
    sin_t = jnp.tile(sin64, (1, LANE // MLA_ROPE))
    return (jnp.concatenate([jnp.ones((n_ctx, LANE), F32), cos_t], axis=0),
            jnp.concatenate([jnp.zeros((n_ctx, LANE), F32), sin_t], axis=0))


def kernel(x, c, ctx, c_ctx, w_mod, b_mod, w_in, w_out, ln_g, ln_b, hg_lb_logits, hg_norm_g,
           mla_q_norm_g, mla_kv_norm_g, mla_w_uq, mla_w_ukv, gla_w_a2, gla_b_a, gla_norm_g):
    B, N, _ = x.shape
    Lc = ctx.shape[1]
    assert N % ROW_TILE == 0 and Lc % ROW_TILE == 0 and N % GRID_W == 0
    cos_t, sin_t = _rope_tables(Lc, N)

    lb_soft = jax.nn.softmax(hg_lb_logits.astype(F32), axis=0)
    hg_lb = jnp.clip(jnp.cumsum(lb_soft, axis=0) - lb_soft[0:1], 0.0, 1.0)

    n_rows = -(-(B + 1) // SUBLANE) * SUBLANE
    c_rows = jnp.zeros((n_rows, D_MODEL), F32).at[0:B].set(c).at[B].set(c_ctx)
    mods = _modulation(c_rows, w_mod, b_mod)

    xc = ctx
    for l in range(DEPTH):
        need_ctx = l < DEPTH - 1
        mod = mods[l].reshape(n_rows, 1, 3 * D_MODEL)
        wuk_p, wvt_p = _pack_wukv(mla_w_ukv[l])
        z_hg, z_gla, mla_gate, q, k, vt = _front(
            x, xc, mod, _pack_w_in(w_in[l]), mla_q_norm_g[l], mla_kv_norm_g[l],
            _pack_wuq(mla_w_uq[l]), wuk_p, wvt_p, cos_t, sin_t)

        hg_f, hg_b = _recurrence(z_hg, "hg", (hg_lb[l],), Lc)
        gl_f, gl_b = _recurrence(z_gla, "gla", _pack_gla_gate(gla_w_a2[l], gla_b_a[l]), Lc)
        att = _attention(q, k, vt, Lc, N, Lc + N)
        att_c = _attention(q, k, vt, 0, Lc, Lc) if need_ctx else None

        x, xc_new = _output_block(x, att, xc if need_ctx else None, att_c, mod, z_hg, mla_gate, z_gla,
                                  hg_f, hg_b, gl_f, gl_b, hg_norm_g[l], gla_norm_g[l],
                                  w_out[l].astype(BF16), ln_g[l], ln_b[l])
        if need_ctx:
            xc = xc_new
    return x
```

```python
import functools
import inspect

import numpy as np
import jax
import jax.numpy as jnp
from jax import lax
from jax.experimental import pallas as pl
from jax.experimental.pallas import tpu as pltpu

F32 = jnp.float32
BF16 = jnp.bfloat16
HIGHEST = lax.Precision.HIGHEST

D_MODEL = 1024
DEPTH = 2
GRID_W = 64
HEADS = 4
HG_DK = 64
GLA_DK = 32
REC_DV = 64
REC_W = HEADS * REC_DV
MLA_NOPE = 128
MLA_ROPE = 64
MLA_DV = 128
MLA_Q_RANK = 256
MLA_KV_RANK = 128
MLA_QK = MLA_NOPE + MLA_ROPE
MLA_W = HEADS * MLA_DV
MLA_SCALE = MLA_QK ** -0.5
GLA_GATE_RANK = 16
GLA_TAU = 16.0
ROPE_BASE = 10000.0
ROPE_FREQS = MLA_ROPE // 4
NORM_EPS = 1e-6
FORGET_MIN = 1e-6
DEEPNORM_ALPHA = (2 * DEPTH) ** 0.25

CHUNK = 64
LANE = 128
SUBLANE = 8
MXU_DEPTH = 256
SUB = SUBLANE
NSUB = CHUNK // SUB
REC_BB = 8
ROW_TILE = 256
ATT_Q_GROUP = 256
ATT_Q_TILE = 2 * ATT_Q_GROUP
LOG2E = 1.4426950408889634
V_ROWS = MLA_DV + 16
HG_COLS = 5 * REC_W
MLA_COLS = 1024
GLA_COLS = 896
IN_COLS = HG_COLS + MLA_COLS + GLA_COLS
MLA_GATE_COL = MLA_COLS - MLA_W

VMEM_LIMIT = 56 * 1024 * 1024


def _cparams(sem):
    return pltpu.CompilerParams(dimension_semantics=sem, vmem_limit_bytes=VMEM_LIMIT)


def _dot(a, b):
    return jnp.dot(a, b, preferred_element_type=F32)


def _dot_nt(a, b):
    return lax.dot_general(a, b, (((1,), (1,)), ((), ())), preferred_element_type=F32)


def _dot_tn(a, b):
    return lax.dot_general(a, b, (((0,), (0,)), ((), ())), preferred_element_type=F32)


def _silu(x):
    return x * jax.nn.sigmoid(x)


def _iota_div(shape, axis, div):
    assert div & (div - 1) == 0
    return lax.shift_right_logical(lax.broadcasted_iota(jnp.int32, shape, axis),
                                   jnp.int32(div.bit_length() - 1))


def _mod_kernel(c_ref, w_ref, b_ref, o_ref):
    s = _silu(c_ref[...])
    o_ref[0] = jnp.dot(s, w_ref[0], precision=HIGHEST, preferred_element_type=F32) + b_ref[0]


def _modulation(c_rows, w_mod, b_mod):
    rows = c_rows.shape[0]
    nt = 3 * D_MODEL // 1024
    return pl.pallas_call(
        _mod_kernel,
        grid=(DEPTH, nt),
        in_specs=[pl.BlockSpec((rows, D_MODEL), lambda l, j: (0, 0)),
                  pl.BlockSpec((1, D_MODEL, 1024), lambda l, j: (l, 0, j)),
                  pl.BlockSpec((1, 1, 1024), lambda l, j: (l, 0, j))],
        out_specs=pl.BlockSpec((1, rows, 1024), lambda l, j: (l, 0, j)),
        out_shape=jax.ShapeDtypeStruct((DEPTH, rows, 3 * D_MODEL), F32),
        compiler_params=_cparams(("arbitrary", "arbitrary")),
        name="modulation",
    )(c_rows, w_mod, b_mod.reshape(DEPTH, 1, 3 * D_MODEL))


def _layer_norm(x):
    mu = jnp.mean(x, axis=-1, keepdims=True)
    xc = x - mu
    var = jnp.mean(xc * xc, axis=-1, keepdims=True)
    return xc * lax.rsqrt(var + NORM_EPS)


def _rms_norm(x, g):
    return x * lax.rsqrt(jnp.mean(x * x, axis=-1, keepdims=True) + NORM_EPS) * g


def _swap_halves(x):
    n = x.shape[-1]
    lane = lax.broadcasted_iota(jnp.int32, x.shape, x.ndim - 1)
    first_half = (lane & (2 * ROPE_FREQS - 1)) < ROPE_FREQS
    return jnp.where(first_half, pltpu.roll(x, n - ROPE_FREQS, x.ndim - 1),
                     pltpu.roll(x, ROPE_FREQS, x.ndim - 1))


def _front_kernel(x_ref, xc_ref, shift_ref, scale_ref, w_ref, qg_ref, kvg_ref, wuq_ref, wuk_ref, wvt_ref,
                  cos_ref, sin_ref, zhg_ref, zgla_ref, mgate_ref, q_ref, k_ref, vt_ref, *, n_ctx_tiles):
    is_ctx = pl.program_id(1) < n_ctx_tiles
    x = jnp.where(is_ctx, xc_ref[0], x_ref[0])
    hb = (_layer_norm(x) * (1.0 + scale_ref[0]) + shift_ref[0]).astype(BF16)
    z = _dot(hb, w_ref[:, HG_COLS:HG_COLS + MLA_COLS])
    zhg_ref[0] = _dot(hb, w_ref[:, 0:HG_COLS]).astype(BF16)
    mgate_ref[0] = z[:, MLA_GATE_COL:].astype(BF16)

    cq = _rms_norm(z[:, 0:MLA_Q_RANK], qg_ref[...]).astype(BF16)
    ckv = _rms_norm(z[:, MLA_Q_RANK:MLA_Q_RANK + MLA_KV_RANK], kvg_ref[...]).astype(BF16)
    kr = z[:, MLA_Q_RANK + MLA_KV_RANK:MLA_Q_RANK + MLA_KV_RANK + LANE]
    qf = _dot(cq, wuq_ref[...]) * (MLA_SCALE * LOG2E)
    kf = _dot(ckv, wuk_ref[...])
    vt = _dot_nt(wvt_ref[...], ckv)
    zgla_ref[0] = _dot(hb, w_ref[:, HG_COLS + MLA_COLS:IN_COLS]).astype(BF16)
    cos = cos_ref[...]
    sin = sin_ref[...]
    q_rope = qf[:, HEADS * MLA_NOPE:]
    q_rope = (q_rope * jnp.concatenate([cos, cos], axis=-1)
              + _swap_halves(q_rope) * jnp.concatenate([sin, sin], axis=-1))
    kr = (kr * cos + _swap_halves(kr) * sin)[:, 0:MLA_ROPE]
    ones = jnp.ones((V_ROWS - MLA_DV, vt.shape[1]), BF16)
    for h in range(HEADS):
        q_ref[0, h, :, 0:MLA_NOPE] = qf[:, h * MLA_NOPE:(h + 1) * MLA_NOPE].astype(BF16)
        q_ref[0, h, :, MLA_NOPE:MLA_QK] = q_rope[:, h * MLA_ROPE:(h + 1) * MLA_ROPE].astype(BF16)
        k_ref[0, h, :, 0:MLA_NOPE] = kf[:, h * MLA_NOPE:(h + 1) * MLA_NOPE].astype(BF16)
        k_ref[0, h, :, MLA_NOPE:MLA_QK] = kr.astype(BF16)
        vt_ref[0, h, 0:MLA_DV, :] = vt[h * MLA_DV:(h + 1) * MLA_DV, :].astype(BF16)
        vt_ref[0, h, MLA_DV:V_ROWS, :] = ones


def _front(x, xc, mod, w_in_packed, q_norm_g, kv_norm_g, wuq_packed, wuk_packed, wvt_packed, cos_t, sin_t):
    B, N, _ = x.shape
    Lc = xc.shape[1]
    T = Lc + N
    tm = ROW_TILE
    nct = Lc // tm
    full = lambda shape: pl.BlockSpec(shape, lambda b, i: (0,) * len(shape))
    mod_row = lambda b, i: jnp.where(i < nct, B, b)
    return pl.pallas_call(
        functools.partial(_front_kernel, n_ctx_tiles=nct),
        grid=(B, T // tm),
        in_specs=[pl.BlockSpec((1, tm, D_MODEL), lambda b, i: (b, jnp.maximum(i - nct, 0), 0)),
                  pl.BlockSpec((1, tm, D_MODEL), lambda b, i: (b, jnp.minimum(i, nct - 1), 0)),
                  pl.BlockSpec((1, 1, D_MODEL), lambda b, i: (mod_row(b, i), 0, 0)),
                  pl.BlockSpec((1, 1, D_MODEL), lambda b, i: (mod_row(b, i), 0, 1)),
                  full((D_MODEL, IN_COLS)), full((1, MLA_Q_RANK)), full((1, MLA_KV_RANK)),
                  full(wuq_packed.shape), full(wuk_packed.shape), full(wvt_packed.shape),
                  pl.BlockSpec((tm, LANE), lambda b, i: (i, 0)),
                  pl.BlockSpec((tm, LANE), lambda b, i: (i, 0))],
        out_specs=[pl.BlockSpec((1, tm, HG_COLS), lambda b, i: (b, i, 0)),
                   pl.BlockSpec((1, tm, GLA_COLS), lambda b, i: (b, i, 0)),
                   pl.BlockSpec((1, tm, MLA_W), lambda b, i: (b, i, 0)),
                   pl.BlockSpec((1, HEADS, tm, MLA_QK),
                                lambda b, i: (b, 0, jnp.where(i < nct, N // tm + i, i - nct), 0)),
                   pl.BlockSpec((1, HEADS, tm, MLA_QK), lambda b, i: (b, 0, i, 0)),
                   pl.BlockSpec((1, HEADS, V_ROWS, tm), lambda b, i: (b, 0, 0, i))],
        out_shape=[jax.ShapeDtypeStruct((B, T, HG_COLS), BF16),
                   jax.ShapeDtypeStruct((B, T, GLA_COLS), BF16),
                   jax.ShapeDtypeStruct((B, T, MLA_W), BF16),
                   jax.ShapeDtypeStruct((B, HEADS, T, MLA_QK), BF16),
                   jax.ShapeDtypeStruct((B, HEADS, T, MLA_QK), BF16),
                   jax.ShapeDtypeStruct((B, HEADS, V_ROWS, T), BF16)],
        compiler_params=_cparams(("arbitrary", "arbitrary")),
        name="front",
    )(x, xc, mod, mod, w_in_packed, q_norm_g.reshape(1, -1), kv_norm_g.reshape(1, -1),
      wuq_packed, wuk_packed, wvt_packed, cos_t, sin_t)


def _block_prefix(x, rev):
    row = lax.broadcasted_iota(jnp.int32, (SUB, x.shape[1]), 0)
    blocks = []
    for I in range(NSUB):
        b = x[I * SUB:(I + 1) * SUB, :]
        step = 1
        while step < SUB:
            if rev:
                b = b + jnp.where(row < SUB - step, pltpu.roll(b, SUB - step, 0), 0.0)
            else:
                b = b + jnp.where(row >= step, pltpu.roll(b, step, 0), 0.0)
            step *= 2
        blocks.append(b)
    return jnp.concatenate(blocks, axis=0)


def _block_prefix_mxu(x, tri):
    hi = x.astype(BF16)
    r1 = x - hi.astype(F32)
    mid = r1.astype(BF16)
    lo = (r1 - mid.astype(F32)).astype(BF16)
    return _dot(tri, hi) + _dot(tri, mid) + _dot(tri, lo)


def _rec_chain(prep, store_out, st_ref, head_sum, tri, rev, dk, min_step_log2):
    dkt = HEADS * dk
    inputs = prep()
    q, k, g, v = (yield from inputs) if inspect.isgenerator(inputs) else inputs
    if tri is None:
        lcum = _block_prefix(g * LOG2E, rev)
    else:
        lcum = _block_prefix_mxu(g * LOG2E, tri)
        yield

    phys = [(NSUB - 1 - p) if rev else p for p in range(NSUB)]
    rows = lambda I: slice(I * SUB, (I + 1) * SUB)
    last_row = [(I * SUB) if rev else (I * SUB + SUB - 1) for I in phys]
    ltot = [jnp.broadcast_to(lcum[r:r + 1, :], (SUB, dkt)) for r in last_row]

    q_in, k_out = [], []
    for p, I in enumerate(phys):
        lc = lcum[rows(I), :]
        q_in.append(q[rows(I), :] * jnp.exp2(lc))
        k_out.append(k[rows(I), :] * jnp.exp2(ltot[p] - lc))

    def running(order):
        sums, acc = {}, None
        for p in order:
            sums[p] = acc
            acc = ltot[p] if acc is None else acc + ltot[p]
        return sums, acc

    pre, total = running(range(NSUB))
    suf, _ = running(reversed(range(NSUB)))
    scaled = lambda x, e: x if e is None else x * jnp.exp2(e)
    q_hat = [scaled(q_in[p], pre[p]) for p in range(NSUB)]
    k_hat = [scaled(k_out[p], suf[p]) for p in range(NSUB)]

    def phys_concat(parts):
        order = sorted(range(NSUB), key=lambda p: phys[p])
        return jnp.concatenate([parts[p] for p in order], axis=0)

    lhs, lhs_at = [], {}
    for p in range(1, NSUB):
        gap = None
        for dist in range(1, p + 1):
            lhs_at[(p, dist)] = len(lhs)
            lhs.append(scaled(q_in[p], gap))
            gap = ltot[p - dist] if gap is None else gap + ltot[p - dist]
    k_out_all = phys_concat(k_out).astype(BF16)
    rk = _iota_div((HEADS * CHUNK, dkt), 0, CHUNK)
    ck = _iota_div((HEADS * CHUNK, dkt), 1, dk)
    kbd_t = jnp.where(rk == ck, jnp.concatenate([k_out_all] * HEADS, axis=0), 0)
    sc = _dot_nt(jnp.concatenate(lhs, axis=0).astype(BF16), kbd_t)

    if min_step_log2 is None:
        ri = lax.broadcasted_iota(jnp.int32, (SUB, dkt), 0)
        units = []
        for I in range(NSUB):
            q_i = q[rows(I), :]
            l_i = lcum[rows(I), :]
            for jj in range(SUB):
                r = I * SUB + jj
                diff = l_i - lcum[r:r + 1, :]
                if jj != (SUB - 1 if rev else 0):
                    diff = jnp.where((ri <= jj) if rev else (ri >= jj), diff, -1e30)
                units.append(q_i * k[r:r + 1, :] * jnp.exp2(diff))
        pack = head_sum.shape[0] // dkt
        packed = [jnp.concatenate(units[n:n + pack], axis=1) for n in range(0, len(units), pack)]
        w_all = _dot(jnp.concatenate(packed, axis=0).astype(BF16), head_sum)
    else:
        assert (SUB // 2) * -min_step_log2 < 120
        q_mid, k_mid = [], []
        for p, I in enumerate(phys):
            m = I * SUB + (SUB // 2 if rev else SUB // 2 - 1)
            rel = lcum[rows(I), :] - jnp.broadcast_to(lcum[m:m + 1, :], (SUB, dkt))
            q_mid.append(q[rows(I), :] * jnp.exp2(rel))
            k_mid.append(k[rows(I), :] * jnp.exp2(-rel))
        k_mid_all = phys_concat(k_mid).astype(BF16)
        kbd_mid_t = jnp.where(rk == ck, jnp.concatenate([k_mid_all] * HEADS, axis=0), 0)
        sc_diag = _dot_nt(phys_concat(q_mid).astype(BF16), kbd_mid_t)

    st = st_ref[...]
    vb = v.astype(BF16)
    o_inter = _dot_nt(phys_concat(q_hat).astype(BF16), st.astype(BF16))
    upd = _dot_tn(vb, phys_concat(k_hat).astype(BF16))
    yield

    lane = lax.broadcasted_iota(jnp.int32, (SUB, HEADS * CHUNK), 1)
    col_blk = lax.shift_right_logical(lane, jnp.int32(SUB.bit_length() - 1)) & (NSUB - 1)
    col_p = (NSUB - 1 - col_blk) if rev else col_blk
    key_pos = lane & (SUB - 1)
    row_pos = lax.broadcasted_iota(jnp.int32, (SUB, HEADS * CHUNK), 0)
    causal = (row_pos <= key_pos) if rev else (row_pos >= key_pos)
    p_rows = []
    for p, I in enumerate(phys):
        if min_step_log2 is None:
            groups = SUB // pack
            level = [w_all[(I * groups + n) * SUB:(I * groups + n + 1) * SUB, :] for n in range(groups)]
            bit = pack
            while len(level) > 1:
                take_odd = (lane & bit) != 0
                level = [jnp.where(take_odd, level[2 * n + 1], level[2 * n]) for n in range(len(level) // 2)]
                bit *= 2
            diag = level[0]
        else:
            diag = jnp.where(causal, sc_diag[rows(I), :], 0.0)
        acc_p = jnp.where(col_p == p, diag, 0.0)
        for dist in range(1, p + 1):
            n = lhs_at[(p, dist)]
            acc_p = jnp.where(col_p == p - dist, sc[n * SUB:(n + 1) * SUB, :], acc_p)
        p_rows.append(acc_p)
    p_all = phys_concat(p_rows).astype(BF16)
    rv = _iota_div((HEADS * CHUNK, REC_W), 0, CHUNK)
    cv = _iota_div((HEADS * CHUNK, REC_W), 1, REC_DV)
    vbd = jnp.where(rv == cv, jnp.concatenate([vb] * HEADS, axis=0), 0)
    o_intra = _dot(p_all, vbd)
    yield

    store_out(o_intra + o_inter)
    rs = _iota_div((REC_W, dkt), 0, REC_DV)
    cs = _iota_div((REC_W, dkt), 1, dk)
    st_ref[...] = st * jnp.exp2(total[0:1, :]) + jnp.where(rs == cs, upd, 0.0)


def _run_interleaved(chains, skew):
    chains = list(chains)
    done = [False] * len(chains)
    started = 0
    while not all(done):
        started = min(started + skew, len(chains))
        for c in range(started):
            if not done[c]:
                try:
                    next(chains[c])
                except StopIteration:
                    done[c] = True


def _store_row(o_ref, i, o):
    o_ref[i] = o.astype(o_ref.dtype)


def _hg_prep(z_ref, i, lb_ref, d):
    col = lambda n: z_ref[i, :, n * REC_W:(n + 1) * REC_W].astype(F32)
    lb = lb_ref[d:d + 1, :]
    q = _silu(col(0))
    v = col(3)
    zz = col(1 + d)
    f = lb + (1.0 - lb) * jax.nn.sigmoid(zz)
    g = jnp.log(jnp.maximum(f, FORGET_MIN))
    k = (1.0 - lb) * jax.nn.sigmoid(-zz)
    return q, k, g, v


def _log_sigmoid(x):
    return jnp.minimum(x, 0.0) - jnp.log(1.0 + jnp.exp(-jnp.abs(x)))


def _gla_prep(z_ref, i, wg_ref, bg_ref, d):
    dkt = HEADS * GLA_DK
    q = z_ref[i, :, 0:dkt].astype(F32) * (GLA_DK ** -0.5)
    k = z_ref[i, :, dkt:2 * dkt].astype(F32)
    v = z_ref[i, :, 2 * dkt:2 * dkt + REC_W].astype(F32)
    logits = _dot(z_ref[i, :, GLA_COLS - LANE:GLA_COLS], wg_ref[d])
    yield
    g = _log_sigmoid(logits + bg_ref[d]) * (1.0 / GLA_TAU)
    return q, k, g, v


def _rec_kernel(*refs, mixer, dk, bb):
    if mixer == "hg":
        zf_ref, zb_ref, lb_ref, hsum_ref, of_ref, ob_ref, stf, stb = refs
        tri_ref = None
    else:
        zf_ref, zb_ref, wg_ref, bg_ref, tri_ref, hsum_ref, of_ref, ob_ref, stf, stb = refs

    @pl.when(pl.program_id(1) == 0)
    def _():
        stf[...] = jnp.zeros_like(stf)
        stb[...] = jnp.zeros_like(stb)

    head_sum = hsum_ref[...]
    min_step_log2 = float(np.log2(FORGET_MIN)) if mixer == "hg" else None
    chains = []
    for i in range(bb):
        for d, (z_ref, o_ref, st) in enumerate(((zf_ref, of_ref, stf), (zb_ref, ob_ref, stb))):
            if mixer == "hg":
                prep = functools.partial(_hg_prep, z_ref, i, lb_ref, d)
            else:
                prep = functools.partial(_gla_prep, z_ref, i, wg_ref, bg_ref, d)
            chains.append(_rec_chain(prep, functools.partial(_store_row, o_ref, i), st.at[i], head_sum,
                                     None if tri_ref is None else tri_ref[d],
                                     rev=(d == 1), dk=dk, min_step_log2=min_step_log2))
    _run_interleaved(chains, skew=1 if mixer == "hg" else len(chains))


def _head_sum_weight(dk):
    dkt = HEADS * dk
    pack = max(1, MXU_DEPTH // dkt)
    w = np.zeros((pack * dkt, HEADS * CHUNK), np.float32)
    for u in range(pack):
        for h in range(HEADS):
            for j in range(u, CHUNK, pack):
                w[u * dkt + h * dk:u * dkt + (h + 1) * dk, h * CHUNK + j] = 1.0
    return jnp.asarray(w, BF16)


def _recurrence(z, mixer, params, n_ctx):
    B, T, cols = z.shape
    dk = HG_DK if mixer == "hg" else GLA_DK
    dkt = HEADS * dk
    nc, ncc = T // CHUNK, n_ctx // CHUNK
    bb = REC_BB if B % REC_BB == 0 else 1
    head_sum = _head_sum_weight(dk)
    full = lambda shape: pl.BlockSpec(shape, lambda b, c: (0,) * len(shape))
    fwd = lambda b, c: (b, c, 0)
    bwd = lambda b, c: (b, jnp.where(c < ncc, ncc - 1 - c, nc + ncc - 1 - c), 0)
    in_specs = [pl.BlockSpec((bb, CHUNK, cols), fwd), pl.BlockSpec((bb, CHUNK, cols), bwd)]
    in_specs += [full(p.shape) for p in params]
    in_specs += [full(head_sum.shape)]
    return pl.pallas_call(
        functools.partial(_rec_kernel, mixer=mixer, dk=dk, bb=bb),
        grid=(B // bb, nc),
        in_specs=in_specs,
        out_specs=[pl.BlockSpec((bb, CHUNK, REC_W), fwd), pl.BlockSpec((bb, CHUNK, REC_W), bwd)],
        out_shape=[jax.ShapeDtypeStruct((B, T, REC_W), BF16)] * 2,
        scratch_shapes=[pltpu.VMEM((bb, REC_W, dkt), F32), pltpu.VMEM((bb, REC_W, dkt), F32)],
        compiler_params=_cparams(("arbitrary", "arbitrary")),
        name="recurrence_" + mixer,
    )(z, z, *params, head_sum)


def _attn_stage(q_ref, k_ref, vt_ref, o_ref, s_w, m_w, s_r, m_r):
    n_keys = k_ref.shape[2]
    half = n_keys // 2
    parts = [slice(0, half), slice(half, n_keys)]
    groups = [slice(j, j + ATT_Q_GROUP) for j in range(0, q_ref.shape[2], ATT_Q_GROUP)]
    for qs in groups:
        q = q_ref[0, 0, qs, :]
        m = None
        for ks in parts:
            s = _dot_nt(k_ref[0, 0, ks, :], q)
            s_w[ks, qs] = s
            mx = jnp.max(s, axis=0, keepdims=True)
            m = mx if m is None else jnp.maximum(m, mx)
        m_w[0:1, qs] = m
        m_prev = m_r[0:1, qs]
        acc = None
        for ks in parts:
            part = _dot(vt_ref[0, 0, :, ks], jnp.exp2(s_r[ks, qs] - m_prev).astype(BF16))
            acc = part if acc is None else acc + part
        o_t = acc[0:MLA_DV, :] * (1.0 / acc[MLA_DV:MLA_DV + 1, :])
        o_ref[0, qs, :] = o_t.T.astype(o_ref.dtype)


def _attn_kernel(q_ref, k_ref, vt_ref, o_ref, s_a, m_a, s_b, m_b):
    t = pl.program_id(0)

    @pl.when(t == 0)
    def _():
        s_b[...] = jnp.zeros_like(s_b)
        m_b[...] = jnp.zeros_like(m_b)

    @pl.when(t % 2 == 0)
    def _():
        _attn_stage(q_ref, k_ref, vt_ref, o_ref, s_a, m_a, s_b, m_b)

    @pl.when(t % 2 == 1)
    def _():
        _attn_stage(q_ref, k_ref, vt_ref, o_ref, s_b, m_b, s_a, m_a)


def _attention(q, k, vt, row0, n_rows, n_keys):
    B, H, _, _ = q.shape
    tq = ATT_Q_TILE if (n_rows % ATT_Q_TILE == 0 and row0 % ATT_Q_TILE == 0) else ATT_Q_GROUP
    assert n_keys % (2 * LANE) == 0 and row0 % tq == 0 and n_rows % tq == 0
    nq, q0 = n_rows // tq, row0 // tq
    n_tiles = B * H * nq

    def tile(t):
        return t // (H * nq), (t // nq) % H, t % nq

    def score_tile(t):
        return tile(jnp.minimum(t, n_tiles - 1))

    def finish_tile(t):
        return tile(jnp.maximum(t - 1, 0))

    def q_map(t):
        b, h, i = score_tile(t)
        return b, h, q0 + i, 0

    def k_map(t):
        b, h, _ = score_tile(t)
        return b, h, 0, 0

    def vt_map(t):
        b, h, _ = finish_tile(t)
        return b, h, 0, 0

    def o_map(t):
        b, h, i = finish_tile(t)
        return b, i, h

    return pl.pallas_call(
        _attn_kernel,
        grid=(n_tiles + 1,),
        in_specs=[pl.BlockSpec((1, 1, tq, MLA_QK), q_map),
                  pl.BlockSpec((1, 1, n_keys, MLA_QK), k_map),
                  pl.BlockSpec((1, 1, V_ROWS, n_keys), vt_map)],
        out_specs=pl.BlockSpec((1, tq, MLA_DV), o_map),
        out_shape=jax.ShapeDtypeStruct((B, n_rows, H * MLA_DV), BF16),
        scratch_shapes=[pltpu.VMEM((n_keys, tq), F32), pltpu.VMEM((SUBLANE, tq), F32),
                        pltpu.VMEM((n_keys, tq), F32), pltpu.VMEM((SUBLANE, tq), F32)],
        compiler_params=_cparams(("arbitrary",)),
        name="attention",
    )(q, k, vt)


def _head_rms(o, ones_bf, g):
    sq = o * o
    hi = sq.astype(BF16)
    lo = (sq - hi.astype(F32)).astype(BF16)
    ms = (_dot(hi, ones_bf) + _dot(lo, ones_bf)) * (1.0 / REC_DV)
    return o * lax.rsqrt(ms + NORM_EPS) * g


def _out_kernel(*refs, n_ctx_tiles):
    if n_ctx_tiles:
        x_ref, att_ref, xc_ref, attc_ref = refs[0:4]
        rest = refs[4:]
    else:
        x_ref, att_ref = refs[0:2]
        rest = refs[2:]
    (gate_ref, zhg_ref, zmla_ref, zgla_ref, hgf_ref, hgb_ref, glf_ref, glb_ref,
     ones_ref, hgn_ref, gln_ref, w_ref, lng_ref, lnb_ref) = rest[0:14]
    x, att = x_ref[0], att_ref[0]
    if n_ctx_tiles:
        is_ctx = pl.program_id(1) < n_ctx_tiles
        x = jnp.where(is_ctx, xc_ref[0], x)
        att = jnp.where(is_ctx, attc_ref[0], att)
    f32 = lambda ref: ref[0].astype(F32)
    ones_bf = ones_ref[...]
    y_hg = _head_rms(f32(hgf_ref) + f32(hgb_ref), ones_bf, hgn_ref[...]) * _silu(f32(zhg_ref))
    y_mla = att.astype(F32) * _silu(f32(zmla_ref))
    y_gla = _head_rms(f32(glf_ref) + f32(glb_ref), ones_bf, gln_ref[...]) * _silu(f32(zgla_ref))
    proj = (_dot(y_hg.astype(BF16), w_ref[0:REC_W, :])
            + _dot(y_mla.astype(BF16), w_ref[REC_W:REC_W + MLA_W, :])
            + _dot(y_gla.astype(BF16), w_ref[REC_W + MLA_W:, :]))
    r = DEEPNORM_ALPHA * x + gate_ref[0] * proj
    res = _layer_norm(r) * lng_ref[...] + lnb_ref[...]
    o_ref = rest[14]
    o_ref[0] = res
    if n_ctx_tiles:
        oc_ref = rest[15]

        @pl.when(is_ctx)
        def _():
            oc_ref[0] = res


def _output_block(x, att, xc, att_c, mod, z_hg, mla_gate, z_gla, hg_f, hg_b, gl_f, gl_b,
                  hg_norm_g, gla_norm_g, w_out_bf, ln_g, ln_b):
    B, N, _ = x.shape
    T = z_hg.shape[1]
    tm = ROW_TILE
    with_ctx = xc is not None
    nct = (T - N) // tm
    first = 0 if with_ctx else nct
    lat = lambda i: jnp.maximum(i + first - nct, 0)
    ones = np.zeros((REC_W, REC_W), np.float32)
    for r in range(REC_W):
        ones[r, (r // REC_DV) * REC_DV:(r // REC_DV + 1) * REC_DV] = 1.0
    ones = jnp.asarray(ones, BF16)
    row = lambda w, col=0: pl.BlockSpec((1, tm, w), lambda b, i: (b, i + first, col))
    lat_row = lambda w: pl.BlockSpec((1, tm, w), lambda b, i: (b, lat(i), 0))
    ctx_row = lambda w: pl.BlockSpec((1, tm, w), lambda b, i: (b, jnp.minimum(i, nct - 1), 0))
    full = lambda shape: pl.BlockSpec(shape, lambda b, i: (0,) * len(shape))
    mod_row = (lambda b, i: jnp.where(i < nct, B, b)) if with_ctx else (lambda b, i: b)
    in_specs = [lat_row(D_MODEL), lat_row(MLA_W)]
    args = [x, att]
    if with_ctx:
        in_specs += [ctx_row(D_MODEL), ctx_row(MLA_W)]
        args += [xc, att_c]
    in_specs += [pl.BlockSpec((1, 1, D_MODEL), lambda b, i: (mod_row(b, i), 0, 2)),
                 row(REC_W, HG_COLS // REC_W - 1),
                 row(MLA_W),
                 row(REC_W, 2),
                 row(REC_W), row(REC_W), row(REC_W), row(REC_W),
                 full((REC_W, REC_W)), full((1, REC_W)), full((1, REC_W)),
                 full((D_MODEL, D_MODEL)), full((1, D_MODEL)), full((1, D_MODEL))]
    args += [mod, z_hg, mla_gate, z_gla, hg_f, hg_b, gl_f, gl_b, ones,
             jnp.tile(hg_norm_g, HEADS).reshape(1, REC_W), jnp.tile(gla_norm_g, HEADS).reshape(1, REC_W),
             w_out_bf, ln_g.reshape(1, D_MODEL), ln_b.reshape(1, D_MODEL)]
    out_specs = [lat_row(D_MODEL)]
    out_shape = [jax.ShapeDtypeStruct((B, N, D_MODEL), F32)]
    if with_ctx:
        out_specs.append(ctx_row(D_MODEL))
        out_shape.append(jax.ShapeDtypeStruct(xc.shape, F32))
    outs = pl.pallas_call(
        functools.partial(_out_kernel, n_ctx_tiles=nct if with_ctx else 0),
        grid=(B, T // tm - first),
        in_specs=in_specs,
        out_specs=out_specs,
        out_shape=out_shape,
        compiler_params=_cparams(("arbitrary", "arbitrary")),
        name="output_block",
    )(*args)
    return (outs[0], outs[1]) if with_ctx else (outs[0], None)


def _pack_w_in(w):
    o = np.cumsum([0, 256, 256, 256, 256, 256, 256, 128, 64, 512, 128, 128, 256, 16, 16, 256])
    col = lambda i: w[:, int(o[i]):int(o[i + 1])]
    zeros = lambda n: jnp.zeros((w.shape[0], n), w.dtype)
    hg = [col(0), col(1), col(2), col(3), col(4)]
    mla = [col(5), col(6), col(7), zeros(64), col(8)]
    gla = [col(9), col(10), col(11), col(14), col(12), col(13), zeros(96)]
    return jnp.concatenate(hg + mla + gla, axis=1).astype(BF16)


def _pack_wuq(w):
    w4 = w.reshape(MLA_Q_RANK, HEADS, MLA_QK)
    return jnp.concatenate([w4[:, :, :MLA_NOPE].reshape(MLA_Q_RANK, -1),
                            w4[:, :, MLA_NOPE:].reshape(MLA_Q_RANK, -1)], axis=1).astype(BF16)


def _pack_wukv(w):
    w4 = w.reshape(MLA_KV_RANK, HEADS, MLA_NOPE + MLA_DV)
    wk = w4[:, :, :MLA_NOPE].reshape(MLA_KV_RANK, -1).astype(BF16)
    wvt = w4[:, :, MLA_NOPE:].reshape(MLA_KV_RANK, -1).T.astype(BF16)
    return wk, wvt


def _pack_gla_gate(w_a2, b_a):
    dkt = HEADS * GLA_DK
    wg = jnp.zeros((2, LANE, dkt), F32)
    wg = wg.at[0, 0:GLA_GATE_RANK].set(w_a2[0])
    wg = wg.at[1, GLA_GATE_RANK:2 * GLA_GATE_RANK].set(w_a2[1])
    return wg.astype(BF16), b_a.reshape(2, 1, dkt)


def _prefix_matrices():
    i = np.arange(CHUNK)
    same = (i[:, None] // SUB) == (i[None, :] // SUB)
    return jnp.asarray(np.stack([same & (i[None, :] <= i[:, None]), same & (i[None, :] >= i[:, None])]), BF16)


def _rope_tables(n_ctx, n):
    rows = n // GRID_W
    pos_r = jnp.repeat(jnp.arange(rows, dtype=F32), GRID_W)
    pos_c = jnp.tile(jnp.arange(GRID_W, dtype=F32), rows)
    inv = 1.0 / (ROPE_BASE ** (jnp.arange(ROPE_FREQS, dtype=F32) / ROPE_FREQS))
    ang = jnp.stack([pos_r, pos_c], axis=-1)[:, :, None] * inv
    cos = jnp.cos(ang)
    sin = jnp.sin(ang)
    cos64 = jnp.concatenate([cos, cos], axis=-1).reshape(n, MLA_ROPE)
    sin64 = jnp.concatenate([-sin, sin], axis=-1).reshape(n, MLA_ROPE)
    cos_t = jnp.tile(cos64, (1, LANE // MLA_ROPE))
    sin_t = jnp.tile(sin64, (1, LANE // MLA_ROPE))
    return (jnp.concatenate([jnp.ones((n_ctx, LANE), F32), cos_t], axis=0),
            jnp.concatenate([jnp.zeros((n_ctx, LANE), F32), sin_t], axis=0))


def kernel(x, c, ctx, c_ctx, w_mod, b_mod, w_in, w_out, ln_g, ln_b, hg_lb_logits, hg_norm_g,
           mla_q_norm_g, mla_kv_norm_g, mla_w_uq, mla_w_ukv, gla_w_a2, gla_b_a, gla_norm_g):
    B, N, _ = x.shape
    Lc = ctx.shape[1]
    assert N % ROW_TILE == 0 and Lc % ROW_TILE == 0 and N % GRID_W == 0
    cos_t, sin_t = _rope_tables(Lc, N)

    lb_soft = jax.nn.softmax(hg_lb_logits.astype(F32), axis=0)
    hg_lb = jnp.clip(jnp.cumsum(lb_soft, axis=0) - lb_soft[0:1], 0.0, 1.0)

    n_rows = -(-(B + 1) // SUBLANE) * SUBLANE
    c_rows = jnp.zeros((n_rows, D_MODEL), F32).at[0:B].set(c).at[B].set(c_ctx)
    mods = _modulation(c_rows, w_mod, b_mod)

    xc = ctx
    for l in range(DEPTH):
        need_ctx = l < DEPTH - 1
        mod = mods[l].reshape(n_rows, 1, 3 * D_MODEL)
        wuk_p, wvt_p = _pack_wukv(mla_w_ukv[l])
        z_hg, z_gla, mla_gate, q, k, vt = _front(
            x, xc, mod, _pack_w_in(w_in[l]), mla_q_norm_g[l], mla_kv_norm_g[l],
            _pack_wuq(mla_w_uq[l]), wuk_p, wvt_p, cos_t, sin_t)

        hg_f, hg_b = _recurrence(z_hg, "hg", (hg_lb[l],), Lc)
        gl_f, gl_b = _recurrence(z_gla, "gla", _pack_gla_gate(gla_w_a2[l], gla_b_a[l]) + (_prefix_matrices(),), Lc)
        att = _attention(q, k, vt, 0, N, Lc + N)
        att_c = _attention(q, k, vt, N, Lc, Lc) if need_ctx else None

        x, xc_new = _output_block(x, att, xc if need_ctx else None, att_c, mod, z_hg, mla_gate, z_gla,
                                  hg_f, hg_b, gl_f, gl_b, hg_norm_g[l], gla_norm_g[l],
                                  w_out[l].astype(BF16), ln_g[l], ln_b[l])
        if need_ctx:
            xc = xc_new
    return x
```

```python
import functools
import inspect

import numpy as np
import jax
import jax.numpy as jnp
from jax import lax
from jax.experimental import pallas as pl
from jax.experimental.pallas import tpu as pltpu

F32 = jnp.float32
BF16 = jnp.bfloat16
HIGHEST = lax.Precision.HIGHEST

D_MODEL = 1024
DEPTH = 2
GRID_W = 64
HEADS = 4
HG_DK = 64
GLA_DK = 32
REC_DV = 64
REC_W = HEADS * REC_DV
MLA_NOPE = 128
MLA_ROPE = 64
MLA_DV = 128
MLA_Q_RANK = 256
MLA_KV_RANK = 128
MLA_QK = MLA_NOPE + MLA_ROPE
MLA_W = HEADS * MLA_DV
MLA_SCALE = MLA_QK ** -0.5
GLA_GATE_RANK = 16
GLA_TAU = 16.0
ROPE_BASE = 10000.0
ROPE_FREQS = MLA_ROPE // 4
NORM_EPS = 1e-6
FORGET_MIN = 1e-6
DEEPNORM_ALPHA = (2 * DEPTH) ** 0.25

CHUNK = 64
LANE = 128
SUBLANE = 8
MXU_DEPTH = 256
SUB = SUBLANE
NSUB = CHUNK // SUB
REC_BB = 8
ROW_TILE = 256
ATT_Q_GROUP = 256
ATT_Q_TILE = 4 * ATT_Q_GROUP
LOG2E = 1.4426950408889634
V_ROWS = MLA_DV + 16
HG_COLS = 5 * REC_W
MLA_COLS = 1024
MISC_COL = MLA_Q_RANK + MLA_KV_RANK
GLA_A_LANE = MLA_ROPE
GLA_PROJ_COLS = 768
GLA_COLS = GLA_PROJ_COLS + LANE
IN_COLS = HG_COLS + MLA_COLS + GLA_PROJ_COLS
MLA_GATE_COL = MLA_COLS - MLA_W

VMEM_LIMIT = 56 * 1024 * 1024


def _cparams(sem):
    return pltpu.CompilerParams(dimension_semantics=sem, vmem_limit_bytes=VMEM_LIMIT)


def _dot(a, b):
    return jnp.dot(a, b, preferred_element_type=F32)


def _dot_nt(a, b):
    return lax.dot_general(a, b, (((1,), (1,)), ((), ())), preferred_element_type=F32)


def _dot_tn(a, b):
    return lax.dot_general(a, b, (((0,), (0,)), ((), ())), preferred_element_type=F32)


def _silu(x):
    return x * jax.nn.sigmoid(x)


def _iota_div(shape, axis, div):
    assert div & (div - 1) == 0
    return lax.shift_right_logical(lax.broadcasted_iota(jnp.int32, shape, axis),
                                   jnp.int32(div.bit_length() - 1))


def _mod_kernel(c_ref, w_ref, b_ref, o_ref):
    s = _silu(c_ref[...])
    o_ref[0] = jnp.dot(s, w_ref[0], precision=HIGHEST, preferred_element_type=F32) + b_ref[0]


def _modulation(c_rows, w_mod, b_mod):
    rows = c_rows.shape[0]
    nt = 3 * D_MODEL // 1024
    return pl.pallas_call(
        _mod_kernel,
        grid=(DEPTH, nt),
        in_specs=[pl.BlockSpec((rows, D_MODEL), lambda l, j: (0, 0)),
                  pl.BlockSpec((1, D_MODEL, 1024), lambda l, j: (l, 0, j)),
                  pl.BlockSpec((1, 1, 1024), lambda l, j: (l, 0, j))],
        out_specs=pl.BlockSpec((1, rows, 1024), lambda l, j: (l, 0, j)),
        out_shape=jax.ShapeDtypeStruct((DEPTH, rows, 3 * D_MODEL), F32),
        compiler_params=_cparams(("arbitrary", "arbitrary")),
        name="modulation",
    )(c_rows, w_mod, b_mod.reshape(DEPTH, 1, 3 * D_MODEL))


def _layer_norm(x):
    mu = jnp.mean(x, axis=-1, keepdims=True)
    xc = x - mu
    var = jnp.mean(xc * xc, axis=-1, keepdims=True)
    return xc * lax.rsqrt(var + NORM_EPS)


def _rms_norm(x, g):
    return x * lax.rsqrt(jnp.mean(x * x, axis=-1, keepdims=True) + NORM_EPS) * g


def _swap_halves(x):
    n = x.shape[-1]
    lane = lax.broadcasted_iota(jnp.int32, x.shape, x.ndim - 1)
    first_half = (lane & (2 * ROPE_FREQS - 1)) < ROPE_FREQS
    return jnp.where(first_half, pltpu.roll(x, n - ROPE_FREQS, x.ndim - 1),
                     pltpu.roll(x, ROPE_FREQS, x.ndim - 1))


def _front_kernel(x_ref, xc_ref, shift_ref, scale_ref, w_ref, qg_ref, kvg_ref, wuq_ref, wuk_ref, wvt_ref,
                  cos_ref, sin_ref, zhg_ref, zgla_ref, mgate_ref, q_ref, k_ref, vt_ref, *, n_ctx_tiles):
    is_ctx = pl.program_id(1) < n_ctx_tiles
    x = jnp.where(is_ctx, xc_ref[0], x_ref[0])
    hb = (_layer_norm(x) * (1.0 + scale_ref[0]) + shift_ref[0]).astype(BF16)
    z = _dot(hb, w_ref[:, HG_COLS:HG_COLS + MLA_COLS])
    zhg_ref[0] = _dot(hb, w_ref[:, 0:HG_COLS]).astype(BF16)
    mgate_ref[0] = z[:, MLA_GATE_COL:].astype(BF16)

    cq = _rms_norm(z[:, 0:MLA_Q_RANK], qg_ref[...]).astype(BF16)
    ckv = _rms_norm(z[:, MLA_Q_RANK:MLA_Q_RANK + MLA_KV_RANK], kvg_ref[...]).astype(BF16)
    kr = z[:, MISC_COL:MISC_COL + LANE]
    zgla_ref[0, :, GLA_PROJ_COLS:GLA_COLS] = kr.astype(BF16)
    qf = _dot(cq, wuq_ref[...]) * (MLA_SCALE * LOG2E)
    kf = _dot(ckv, wuk_ref[...])
    vt = _dot_nt(wvt_ref[...], ckv)
    zgla_ref[0, :, 0:GLA_PROJ_COLS] = _dot(hb, w_ref[:, HG_COLS + MLA_COLS:IN_COLS]).astype(BF16)
    cos = cos_ref[...]
    sin = sin_ref[...]
    q_rope = qf[:, HEADS * MLA_NOPE:]
    q_rope = (q_rope * jnp.concatenate([cos, cos], axis=-1)
              + _swap_halves(q_rope) * jnp.concatenate([sin, sin], axis=-1))
    kr = (kr * cos + _swap_halves(kr) * sin)[:, 0:MLA_ROPE]
    ones = jnp.ones((V_ROWS - MLA_DV, vt.shape[1]), BF16)
    for h in range(HEADS):
        q_ref[0, h, :, 0:MLA_NOPE] = qf[:, h * MLA_NOPE:(h + 1) * MLA_NOPE].astype(BF16)
        q_ref[0, h, :, MLA_NOPE:MLA_QK] = q_rope[:, h * MLA_ROPE:(h + 1) * MLA_ROPE].astype(BF16)
        k_ref[0, h, :, 0:MLA_NOPE] = kf[:, h * MLA_NOPE:(h + 1) * MLA_NOPE].astype(BF16)
        k_ref[0, h, :, MLA_NOPE:MLA_QK] = kr.astype(BF16)
        vt_ref[0, h, 0:MLA_DV, :] = vt[h * MLA_DV:(h + 1) * MLA_DV, :].astype(BF16)
        vt_ref[0, h, MLA_DV:V_ROWS, :] = ones


def _front(x, xc, mod, w_in_packed, q_norm_g, kv_norm_g, wuq_packed, wuk_packed, wvt_packed, cos_t, sin_t):
    B, N, _ = x.shape
    Lc = xc.shape[1]
    T = Lc + N
    tm = ROW_TILE
    nct = Lc // tm
    full = lambda shape: pl.BlockSpec(shape, lambda b, i: (0,) * len(shape))
    mod_row = lambda b, i: jnp.where(i < nct, B, b)
    return pl.pallas_call(
        functools.partial(_front_kernel, n_ctx_tiles=nct),
        grid=(B, T // tm),
        in_specs=[pl.BlockSpec((1, tm, D_MODEL), lambda b, i: (b, jnp.maximum(i - nct, 0), 0)),
                  pl.BlockSpec((1, tm, D_MODEL), lambda b, i: (b, jnp.minimum(i, nct - 1), 0)),
                  pl.BlockSpec((1, 1, D_MODEL), lambda b, i: (mod_row(b, i), 0, 0)),
                  pl.BlockSpec((1, 1, D_MODEL), lambda b, i: (mod_row(b, i), 0, 1)),
                  full((D_MODEL, IN_COLS)), full((1, MLA_Q_RANK)), full((1, MLA_KV_RANK)),
                  full(wuq_packed.shape), full(wuk_packed.shape), full(wvt_packed.shape),
                  pl.BlockSpec((tm, LANE), lambda b, i: (i, 0)),
                  pl.BlockSpec((tm, LANE), lambda b, i: (i, 0))],
        out_specs=[pl.BlockSpec((1, tm, HG_COLS), lambda b, i: (b, i, 0)),
                   pl.BlockSpec((1, tm, GLA_COLS), lambda b, i: (b, i, 0)),
                   pl.BlockSpec((1, tm, MLA_W), lambda b, i: (b, i, 0)),
                   pl.BlockSpec((1, HEADS, tm, MLA_QK),
                                lambda b, i: (b, 0, jnp.where(i < nct, N // tm + i, i - nct), 0)),
                   pl.BlockSpec((1, HEADS, tm, MLA_QK), lambda b, i: (b, 0, i, 0)),
                   pl.BlockSpec((1, HEADS, V_ROWS, tm), lambda b, i: (b, 0, 0, i))],
        out_shape=[jax.ShapeDtypeStruct((B, T, HG_COLS), BF16),
                   jax.ShapeDtypeStruct((B, T, GLA_COLS), BF16),
                   jax.ShapeDtypeStruct((B, T, MLA_W), BF16),
                   jax.ShapeDtypeStruct((B, HEADS, T, MLA_QK), BF16),
                   jax.ShapeDtypeStruct((B, HEADS, T, MLA_QK), BF16),
                   jax.ShapeDtypeStruct((B, HEADS, V_ROWS, T), BF16)],
        compiler_params=_cparams(("arbitrary", "arbitrary")),
        name="front",
    )(x, xc, mod, mod, w_in_packed, q_norm_g.reshape(1, -1), kv_norm_g.reshape(1, -1),
      wuq_packed, wuk_packed, wvt_packed, cos_t, sin_t)


def _block_prefix(x, rev):
    row = lax.broadcasted_iota(jnp.int32, (SUB, x.shape[1]), 0)
    blocks = []
    for I in range(NSUB):
        b = x[I * SUB:(I + 1) * SUB, :]
        step = 1
        while step < SUB:
            if rev:
                b = b + jnp.where(row < SUB - step, pltpu.roll(b, SUB - step, 0), 0.0)
            else:
                b = b + jnp.where(row >= step, pltpu.roll(b, step, 0), 0.0)
            step *= 2
        blocks.append(b)
    return jnp.concatenate(blocks, axis=0)


def _block_prefix_mxu(x, tri):
    hi = x.astype(BF16)
    r1 = x - hi.astype(F32)
    mid = r1.astype(BF16)
    lo = (r1 - mid.astype(F32)).astype(BF16)
    return _dot(tri, hi) + _dot(tri, mid) + _dot(tri, lo)


def _rec_chain(prep, store_out, st_ref, head_sum, tri, rev, dk, min_step_log2):
    dkt = HEADS * dk
    inputs = prep()
    q, k, g, v = (yield from inputs) if inspect.isgenerator(inputs) else inputs
    if tri is None:
        lcum = _block_prefix(g, rev)
    else:
        lcum = _block_prefix_mxu(g, tri)
        yield

    phys = [(NSUB - 1 - p) if rev else p for p in range(NSUB)]
    rows = lambda I: slice(I * SUB, (I + 1) * SUB)
    last_row = [(I * SUB) if rev else (I * SUB + SUB - 1) for I in phys]
    ltot = [jnp.broadcast_to(lcum[r:r + 1, :], (SUB, dkt)) for r in last_row]

    q_in, k_out = [], []
    for p, I in enumerate(phys):
        lc = lcum[rows(I), :]
        q_in.append(q[rows(I), :] * jnp.exp2(lc))
        k_out.append(k[rows(I), :] * jnp.exp2(ltot[p] - lc))

    def running(order):
        sums, acc = {}, None
        for p in order:
            sums[p] = acc
            acc = ltot[p] if acc is None else acc + ltot[p]
        return sums, acc

    pre, total = running(range(NSUB))
    suf, _ = running(reversed(range(NSUB)))
    scaled = lambda x, e: x if e is None else x * jnp.exp2(e)
    q_hat = [scaled(q_in[p], pre[p]) for p in range(NSUB)]
    k_hat = [scaled(k_out[p], suf[p]) for p in range(NSUB)]

    def phys_concat(parts):
        order = sorted(range(NSUB), key=lambda p: phys[p])
        return jnp.concatenate([parts[p] for p in order], axis=0)

    lhs, lhs_at = [], {}
    for p in range(1, NSUB):
        gap = None
        for dist in range(1, p + 1):
            lhs_at[(p, dist)] = len(lhs)
            lhs.append(scaled(q_in[p], gap))
            gap = ltot[p - dist] if gap is None else gap + ltot[p - dist]
    k_out_all = phys_concat(k_out).astype(BF16)
    rk = _iota_div((HEADS * CHUNK, dkt), 0, CHUNK)
    ck = _iota_div((HEADS * CHUNK, dkt), 1, dk)
    kbd_t = jnp.where(rk == ck, jnp.concatenate([k_out_all] * HEADS, axis=0), 0)
    sc = _dot_nt(jnp.concatenate(lhs, axis=0).astype(BF16), kbd_t)

    if min_step_log2 is None:
        ri = lax.broadcasted_iota(jnp.int32, (SUB, dkt), 0)
        units = []
        for I in range(NSUB):
            q_i = q[rows(I), :]
            l_i = lcum[rows(I), :]
            for jj in range(SUB):
                r = I * SUB + jj
                diff = l_i - lcum[r:r + 1, :]
                if jj != (SUB - 1 if rev else 0):
                    diff = jnp.where((ri <= jj) if rev else (ri >= jj), diff, -1e30)
                units.append(q_i * k[r:r + 1, :] * jnp.exp2(diff))
        pack = head_sum.shape[0] // dkt
        packed = [jnp.concatenate(units[n:n + pack], axis=1) for n in range(0, len(units), pack)]
        w_all = _dot(jnp.concatenate(packed, axis=0).astype(BF16), head_sum)
    else:
        assert (SUB // 2) * -min_step_log2 < 120
        q_mid, k_mid = [], []
        for p, I in enumerate(phys):
            m = I * SUB + (SUB // 2 if rev else SUB // 2 - 1)
            rel = lcum[rows(I), :] - jnp.broadcast_to(lcum[m:m + 1, :], (SUB, dkt))
            q_mid.append(q[rows(I), :] * jnp.exp2(rel))
            k_mid.append(k[rows(I), :] * jnp.exp2(-rel))
        k_mid_all = phys_concat(k_mid).astype(BF16)
        kbd_mid_t = jnp.where(rk == ck, jnp.concatenate([k_mid_all] * HEADS, axis=0), 0)
        sc_diag = _dot_nt(phys_concat(q_mid).astype(BF16), kbd_mid_t)

    st = st_ref[...]
    vb = v.astype(BF16)
    rs = _iota_div((REC_W, dkt), 0, REC_DV)
    cs = _iota_div((REC_W, dkt), 1, dk)
    o_inter = _dot_nt(phys_concat(q_hat).astype(BF16), jnp.where(rs == cs, st.astype(BF16), 0))
    upd = _dot_tn(vb, phys_concat(k_hat).astype(BF16))
    yield

    lane = lax.broadcasted_iota(jnp.int32, (SUB, HEADS * CHUNK), 1)
    col_blk = lax.shift_right_logical(lane, jnp.int32(SUB.bit_length() - 1)) & (NSUB - 1)
    col_p = (NSUB - 1 - col_blk) if rev else col_blk
    key_pos = lane & (SUB - 1)
    row_pos = lax.broadcasted_iota(jnp.int32, (SUB, HEADS * CHUNK), 0)
    causal = (row_pos <= key_pos) if rev else (row_pos >= key_pos)
    p_rows = []
    for p, I in enumerate(phys):
        if min_step_log2 is None:
            groups = SUB // pack
            level = [w_all[(I * groups + n) * SUB:(I * groups + n + 1) * SUB, :] for n in range(groups)]
            bit = pack
            while len(level) > 1:
                take_odd = (lane & bit) != 0
                level = [jnp.where(take_odd, level[2 * n + 1], level[2 * n]) for n in range(len(level) // 2)]
                bit *= 2
            diag = level[0]
        else:
            diag = jnp.where(causal, sc_diag[rows(I), :], 0.0)
        acc_p = jnp.where(col_p == p, diag, 0.0)
        for dist in range(1, p + 1):
            n = lhs_at[(p, dist)]
            acc_p = jnp.where(col_p == p - dist, sc[n * SUB:(n + 1) * SUB, :], acc_p)
        p_rows.append(acc_p)
    p_all = phys_concat(p_rows).astype(BF16)
    rv = _iota_div((HEADS * CHUNK, REC_W), 0, CHUNK)
    cv = _iota_div((HEADS * CHUNK, REC_W), 1, REC_DV)
    vbd = jnp.where(rv == cv, jnp.concatenate([vb] * HEADS, axis=0), 0)
    o_intra = _dot(p_all, vbd)
    yield

    store_out(o_intra + o_inter)
    st_ref[...] = st * jnp.exp2(total[0:1, :]) + upd


def _run_interleaved(chains, skew):
    chains = list(chains)
    done = [False] * len(chains)
    started = 0
    while not all(done):
        started = min(started + skew, len(chains))
        for c in range(started):
            if not done[c]:
                try:
                    next(chains[c])
                except StopIteration:
                    done[c] = True


def _store_row(o_ref, i, o):
    o_ref[i] = o.astype(o_ref.dtype)


def _hg_prep(z_ref, i, lb_ref, d):
    col = lambda n: z_ref[i, :, n * REC_W:(n + 1) * REC_W].astype(F32)
    lb = lb_ref[d:d + 1, :]
    q = _silu(col(0))
    v = col(3)
    zz = col(1 + d)
    f = lb + (1.0 - lb) * jax.nn.sigmoid(zz)
    g = jnp.log2(jnp.maximum(f, FORGET_MIN))
    k = (1.0 - lb) * jax.nn.sigmoid(-zz)
    return q, k, g, v


def _log_sigmoid(x):
    return jnp.minimum(x, 0.0) - jnp.log(1.0 + jnp.exp(-jnp.abs(x)))


def _gla_prep(z_ref, i, wg_ref, bg_ref, d):
    dkt = HEADS * GLA_DK
    q = z_ref[i, :, 0:dkt].astype(F32) * (GLA_DK ** -0.5)
    k = z_ref[i, :, dkt:2 * dkt].astype(F32)
    v = z_ref[i, :, 2 * dkt:2 * dkt + REC_W].astype(F32)
    logits = _dot(z_ref[i, :, GLA_COLS - LANE:GLA_COLS], wg_ref[d])
    yield
    g = _log_sigmoid(logits + bg_ref[d]) * (LOG2E / GLA_TAU)
    return q, k, g, v


def _rec_kernel(*refs, mixer, dk, bb):
    if mixer == "hg":
        zf_ref, zb_ref, lb_ref, hsum_ref, of_ref, ob_ref, stf, stb = refs
        tri_ref = None
    else:
        zf_ref, zb_ref, wg_ref, bg_ref, tri_ref, hsum_ref, of_ref, ob_ref, stf, stb = refs

    @pl.when(pl.program_id(1) == 0)
    def _():
        stf[...] = jnp.zeros_like(stf)
        stb[...] = jnp.zeros_like(stb)

    head_sum = hsum_ref[...]
    min_step_log2 = float(np.log2(FORGET_MIN)) if mixer == "hg" else None
    chains = []
    for i in range(bb):
        for d, (z_ref, o_ref, st) in enumerate(((zf_ref, of_ref, stf), (zb_ref, ob_ref, stb))):
            if mixer == "hg":
                prep = functools.partial(_hg_prep, z_ref, i, lb_ref, d)
            else:
                prep = functools.partial(_gla_prep, z_ref, i, wg_ref, bg_ref, d)
            chains.append(_rec_chain(prep, functools.partial(_store_row, o_ref, i), st.at[i], head_sum,
                                     None if tri_ref is None else tri_ref[d],
                                     rev=(d == 1), dk=dk, min_step_log2=min_step_log2))
    _run_interleaved(chains, skew=1 if mixer == "hg" else len(chains))


def _head_sum_weight(dk):
    dkt = HEADS * dk
    pack = max(1, MXU_DEPTH // dkt)
    w = np.zeros((pack * dkt, HEADS * CHUNK), np.float32)
    for u in range(pack):
        for h in range(HEADS):
            for j in range(u, CHUNK, pack):
                w[u * dkt + h * dk:u * dkt + (h + 1) * dk, h * CHUNK + j] = 1.0
    return jnp.asarray(w, BF16)


def _recurrence(z, mixer, params, n_ctx):
    B, T, cols = z.shape
    dk = HG_DK if mixer == "hg" else GLA_DK
    dkt = HEADS * dk
    nc, ncc = T // CHUNK, n_ctx // CHUNK
    bb = REC_BB if B % REC_BB == 0 else 1
    head_sum = _head_sum_weight(dk)
    full = lambda shape: pl.BlockSpec(shape, lambda b, c: (0,) * len(shape))
    fwd = lambda b, c: (b, c, 0)
    bwd = lambda b, c: (b, jnp.where(c < ncc, ncc - 1 - c, nc + ncc - 1 - c), 0)
    in_specs = [pl.BlockSpec((bb, CHUNK, cols), fwd), pl.BlockSpec((bb, CHUNK, cols), bwd)]
    in_specs += [full(p.shape) for p in params]
    in_specs += [full(head_sum.shape)]
    return pl.pallas_call(
        functools.partial(_rec_kernel, mixer=mixer, dk=dk, bb=bb),
        grid=(B // bb, nc),
        in_specs=in_specs,
        out_specs=[pl.BlockSpec((bb, CHUNK, REC_W), fwd), pl.BlockSpec((bb, CHUNK, REC_W), bwd)],
        out_shape=[jax.ShapeDtypeStruct((B, T, REC_W), BF16)] * 2,
        scratch_shapes=[pltpu.VMEM((bb, REC_W, dkt), F32), pltpu.VMEM((bb, REC_W, dkt), F32)],
        compiler_params=_cparams(("arbitrary", "arbitrary")),
        name="recurrence_" + mixer,
    )(z, z, *params, head_sum)


def _attn_stage(q_ref, k_ref, vt_ref, o_ref, s_w, m_w, s_r, m_r):
    n_keys = k_ref.shape[2]
    half = n_keys // 2
    parts = [slice(0, half), slice(half, n_keys)]
    groups = [slice(j, j + ATT_Q_GROUP) for j in range(0, q_ref.shape[2], ATT_Q_GROUP)]
    for qs in groups:
        q = q_ref[0, 0, qs, :]
        m = None
        for ks in parts:
            s = _dot_nt(k_ref[0, 0, ks, :], q)
            s_w[ks, qs] = s
            mx = jnp.max(s, axis=0, keepdims=True)
            m = mx if m is None else jnp.maximum(m, mx)
        m_w[0:1, qs] = m
        m_prev = m_r[0:1, qs]
        acc = None
        for ks in parts:
            part = _dot(vt_ref[0, 0, :, ks], jnp.exp2(s_r[ks, qs] - m_prev).astype(BF16))
            acc = part if acc is None else acc + part
        o_t = acc[0:MLA_DV, :] * (1.0 / acc[MLA_DV:MLA_DV + 1, :])
        o_ref[0, qs, :] = o_t.T.astype(o_ref.dtype)


def _attn_kernel(q_ref, k_ref, vt_ref, o_ref, s_a, m_a, s_b, m_b):
    t = pl.program_id(0)

    @pl.when(t == 0)
    def _():
        s_b[...] = jnp.zeros_like(s_b)
        m_b[...] = jnp.zeros_like(m_b)

    @pl.when(t % 2 == 0)
    def _():
        _attn_stage(q_ref, k_ref, vt_ref, o_ref, s_a, m_a, s_b, m_b)

    @pl.when(t % 2 == 1)
    def _():
        _attn_stage(q_ref, k_ref, vt_ref, o_ref, s_b, m_b, s_a, m_a)


def _attention(q, k, vt, row0, n_rows, n_keys):
    B, H, _, _ = q.shape
    tq = ATT_Q_TILE if (n_rows % ATT_Q_TILE == 0 and row0 % ATT_Q_TILE == 0) else ATT_Q_GROUP
    assert n_keys % (2 * LANE) == 0 and row0 % tq == 0 and n_rows % tq == 0
    nq, q0 = n_rows // tq, row0 // tq
    n_tiles = B * H * nq

    def tile(t):
        return t // (H * nq), (t // nq) % H, t % nq

    def score_tile(t):
        return tile(jnp.minimum(t, n_tiles - 1))

    def finish_tile(t):
        return tile(jnp.maximum(t - 1, 0))

    def q_map(t):
        b, h, i = score_tile(t)
        return b, h, q0 + i, 0

    def k_map(t):
        b, h, _ = score_tile(t)
        return b, h, 0, 0

    def vt_map(t):
        b, h, _ = finish_tile(t)
        return b, h, 0, 0

    def o_map(t):
        b, h, i = finish_tile(t)
        return b, i, h

    return pl.pallas_call(
        _attn_kernel,
        grid=(n_tiles + 1,),
        in_specs=[pl.BlockSpec((1, 1, tq, MLA_QK), q_map),
                  pl.BlockSpec((1, 1, n_keys, MLA_QK), k_map),
                  pl.BlockSpec((1, 1, V_ROWS, n_keys), vt_map)],
        out_specs=pl.BlockSpec((1, tq, MLA_DV), o_map),
        out_shape=jax.ShapeDtypeStruct((B, n_rows, H * MLA_DV), BF16),
        scratch_shapes=[pltpu.VMEM((n_keys, tq), F32), pltpu.VMEM((SUBLANE, tq), F32),
                        pltpu.VMEM((n_keys, tq), F32), pltpu.VMEM((SUBLANE, tq), F32)],
        compiler_params=_cparams(("arbitrary",)),
        name="attention",
    )(q, k, vt)


def _head_rms(o, ones_bf, g):
    ms = _dot((o * o).astype(BF16), ones_bf) * (1.0 / REC_DV)
    return o * lax.rsqrt(ms + NORM_EPS) * g


def _out_kernel(*refs, n_ctx_tiles):
    if n_ctx_tiles:
        x_ref, att_ref, xc_ref, attc_ref = refs[0:4]
        rest = refs[4:]
    else:
        x_ref, att_ref = refs[0:2]
        rest = refs[2:]
    (gate_ref, zhg_ref, zmla_ref, zgla_ref, hgf_ref, hgb_ref, glf_ref, glb_ref,
     ones_ref, hgn_ref, gln_ref, w_ref, lng_ref, lnb_ref) = rest[0:14]
    x, att = x_ref[0], att_ref[0]
    if n_ctx_tiles:
        is_ctx = pl.program_id(1) < n_ctx_tiles
        x = jnp.where(is_ctx, xc_ref[0], x)
        att = jnp.where(is_ctx, attc_ref[0], att)
    f32 = lambda ref: ref[0].astype(F32)
    ones_bf = ones_ref[...]
    y_hg = _head_rms(f32(hgf_ref) + f32(hgb_ref), ones_bf, hgn_ref[...]) * _silu(f32(zhg_ref))
    y_mla = att.astype(F32) * _silu(f32(zmla_ref))
    y_gla = _head_rms(f32(glf_ref) + f32(glb_ref), ones_bf, gln_ref[...]) * _silu(f32(zgla_ref))
    proj = (_dot(y_hg.astype(BF16), w_ref[0:REC_W, :])
            + _dot(y_mla.astype(BF16), w_ref[REC_W:REC_W + MLA_W, :])
            + _dot(y_gla.astype(BF16), w_ref[REC_W + MLA_W:, :]))
    r = DEEPNORM_ALPHA * x + gate_ref[0] * proj
    res = _layer_norm(r) * lng_ref[...] + lnb_ref[...]
    o_ref = rest[14]
    o_ref[0] = res
    if n_ctx_tiles:
        oc_ref = rest[15]

        @pl.when(is_ctx)
        def _():
            oc_ref[0] = res


def _output_block(x, att, xc, att_c, mod, z_hg, mla_gate, z_gla, hg_f, hg_b, gl_f, gl_b,
                  hg_norm_g, gla_norm_g, w_out_bf, ln_g, ln_b):
    B, N, _ = x.shape
    T = z_hg.shape[1]
    tm = ROW_TILE
    with_ctx = xc is not None
    nct = (T - N) // tm
    first = 0 if with_ctx else nct
    lat = lambda i: jnp.maximum(i + first - nct, 0)
    ones = np.zeros((REC_W, REC_W), np.float32)
    for r in range(REC_W):
        ones[r, (r // REC_DV) * REC_DV:(r // REC_DV + 1) * REC_DV] = 1.0
    ones = jnp.asarray(ones, BF16)
    row = lambda w, col=0: pl.BlockSpec((1, tm, w), lambda b, i: (b, i + first, col))
    lat_row = lambda w: pl.BlockSpec((1, tm, w), lambda b, i: (b, lat(i), 0))
    ctx_row = lambda w: pl.BlockSpec((1, tm, w), lambda b, i: (b, jnp.minimum(i, nct - 1), 0))
    full = lambda shape: pl.BlockSpec(shape, lambda b, i: (0,) * len(shape))
    mod_row = (lambda b, i: jnp.where(i < nct, B, b)) if with_ctx else (lambda b, i: b)
    in_specs = [lat_row(D_MODEL), lat_row(MLA_W)]
    args = [x, att]
    if with_ctx:
        in_specs += [ctx_row(D_MODEL), ctx_row(MLA_W)]
        args += [xc, att_c]
    in_specs += [pl.BlockSpec((1, 1, D_MODEL), lambda b, i: (mod_row(b, i), 0, 2)),
                 row(REC_W, HG_COLS // REC_W - 1),
                 row(MLA_W),
                 row(REC_W, 2),
                 row(REC_W), row(REC_W), row(REC_W), row(REC_W),
                 full((REC_W, REC_W)), full((1, REC_W)), full((1, REC_W)),
                 full((D_MODEL, D_MODEL)), full((1, D_MODEL)), full((1, D_MODEL))]
    args += [mod, z_hg, mla_gate, z_gla, hg_f, hg_b, gl_f, gl_b, ones,
             jnp.tile(hg_norm_g, HEADS).reshape(1, REC_W), jnp.tile(gla_norm_g, HEADS).reshape(1, REC_W),
             w_out_bf, ln_g.reshape(1, D_MODEL), ln_b.reshape(1, D_MODEL)]
    out_specs = [lat_row(D_MODEL)]
    out_shape = [jax.ShapeDtypeStruct((B, N, D_MODEL), F32)]
    if with_ctx:
        out_specs.append(ctx_row(D_MODEL))
        out_shape.append(jax.ShapeDtypeStruct(xc.shape, F32))
    outs = pl.pallas_call(
        functools.partial(_out_kernel, n_ctx_tiles=nct if with_ctx else 0),
        grid=(B, T // tm - first),
        in_specs=in_specs,
        out_specs=out_specs,
        out_shape=out_shape,
        compiler_params=_cparams(("arbitrary", "arbitrary")),
        name="output_block",
    )(*args)
    return (outs[0], outs[1]) if with_ctx else (outs[0], None)


def _pack_w_in(w):
    o = np.cumsum([0, 256, 256, 256, 256, 256, 256, 128, 64, 512, 128, 128, 256, 16, 16, 256])
    col = lambda i: w[:, int(o[i]):int(o[i + 1])]
    zeros = lambda n: jnp.zeros((w.shape[0], n), w.dtype)
    hg = [col(0), col(1), col(2), col(3), col(4)]
    mla = [col(5), col(6), col(7), col(12), col(13), zeros(LANE - MLA_ROPE - 2 * GLA_GATE_RANK), col(8)]
    gla = [col(9), col(10), col(11), col(14)]
    return jnp.concatenate(hg + mla + gla, axis=1).astype(BF16)


def _pack_wuq(w):
    w4 = w.reshape(MLA_Q_RANK, HEADS, MLA_QK)
    return jnp.concatenate([w4[:, :, :MLA_NOPE].reshape(MLA_Q_RANK, -1),
                            w4[:, :, MLA_NOPE:].reshape(MLA_Q_RANK, -1)], axis=1).astype(BF16)


def _pack_wukv(w):
    w4 = w.reshape(MLA_KV_RANK, HEADS, MLA_NOPE + MLA_DV)
    wk = w4[:, :, :MLA_NOPE].reshape(MLA_KV_RANK, -1).astype(BF16)
    wvt = w4[:, :, MLA_NOPE:].reshape(MLA_KV_RANK, -1).T.astype(BF16)
    return wk, wvt


def _pack_gla_gate(w_a2, b_a):
    dkt = HEADS * GLA_DK
    wg = jnp.zeros((2, LANE, dkt), F32)
    wg = wg.at[0, GLA_A_LANE:GLA_A_LANE + GLA_GATE_RANK].set(w_a2[0])
    wg = wg.at[1, GLA_A_LANE + GLA_GATE_RANK:GLA_A_LANE + 2 * GLA_GATE_RANK].set(w_a2[1])
    return wg.astype(BF16), b_a.reshape(2, 1, dkt)


def _prefix_matrices():
    i = np.arange(CHUNK)
    same = (i[:, None] // SUB) == (i[None, :] // SUB)
    return jnp.asarray(np.stack([same & (i[None, :] <= i[:, None]), same & (i[None, :] >= i[:, None])]), BF16)


def _rope_tables(n_ctx, n):
    rows = n // GRID_W
    pos_r = jnp.repeat(jnp.arange(rows, dtype=F32), GRID_W)
    pos_c = jnp.tile(jnp.arange(GRID_W, dtype=F32), rows)
    inv = 1.0 / (ROPE_BASE ** (jnp.arange(ROPE_FREQS, dtype=F32) / ROPE_FREQS))
    ang = jnp.stack([pos_r, pos_c], axis=-1)[:, :, None] * inv
    cos = jnp.cos(ang)
    sin = jnp.sin(ang)
    cos64 = jnp.concatenate([cos, cos], axis=-1).reshape(n, MLA_ROPE)
    sin64 = jnp.concatenate([-sin, sin], axis=-1).reshape(n, MLA_ROPE)
    cos_t = jnp.tile(cos64, (1, LANE // MLA_ROPE))
    sin_t = jnp.tile(sin64, (1, LANE // MLA_ROPE))
    return (jnp.concatenate([jnp.ones((n_ctx, LANE), F32), cos_t], axis=0),
            jnp.concatenate([jnp.zeros((n_ctx, LANE), F32), sin_t], axis=0))


def kernel(x, c, ctx, c_ctx, w_mod, b_mod, w_in, w_out, ln_g, ln_b, hg_lb_logits, hg_norm_g,
           mla_q_norm_g, mla_kv_norm_g, mla_w_uq, mla_w_ukv, gla_w_a2, gla_b_a, gla_norm_g):
    B, N, _ = x.shape
    Lc = ctx.shape[1]
    assert N % ROW_TILE == 0 and Lc % ROW_TILE == 0 and N % GRID_W == 0
    cos_t, sin_t = _rope_tables(Lc, N)

    lb_soft = jax.nn.softmax(hg_lb_logits.astype(F32), axis=0)
    hg_lb = jnp.clip(jnp.cumsum(lb_soft, axis=0) - lb_soft[0:1], 0.0, 1.0)

    n_rows = -(-(B + 1) // SUBLANE) * SUBLANE
    c_rows = jnp.zeros((n_rows, D_MODEL), F32).at[0:B].set(c).at[B].set(c_ctx)
    mods = _modulation(c_rows, w_mod, b_mod)

    xc = ctx
    for l in range(DEPTH):
        need_ctx = l < DEPTH - 1
        mod = mods[l].reshape(n_rows, 1, 3 * D_MODEL)
        wuk_p, wvt_p = _pack_wukv(mla_w_ukv[l])
        z_hg, z_gla, mla_gate, q, k, vt = _front(
            x, xc, mod, _pack_w_in(w_in[l]), mla_q_norm_g[l], mla_kv_norm_g[l],
            _pack_wuq(mla_w_uq[l]), wuk_p, wvt_p, cos_t, sin_t)

        hg_f, hg_b = _recurrence(z_hg, "hg", (hg_lb[l],), Lc)
        gl_f, gl_b = _recurrence(z_gla, "gla", _pack_gla_gate(gla_w_a2[l], gla_b_a[l]) + (_prefix_matrices(),), Lc)
        att = _attention(q, k, vt, 0, N, Lc + N)
        att_c = _attention(q, k, vt, N, Lc, Lc) if need_ctx else None

        x, xc_new = _output_block(x, att, xc if need_ctx else None, att_c, mod, z_hg, mla_gate, z_gla,
                                  hg_f, hg_b, gl_f, gl_b, hg_norm_g[l], gla_norm_g[l],
                                  w_out[l].astype(BF16), ln_g[l], ln_b[l])
        if need_ctx:
            xc = xc_new
    return x
```

```python
import functools
import inspect

import numpy as np
import jax
import jax.numpy as jnp
from jax import lax
from jax.experimental import pallas as pl
from jax.experimental.pallas import tpu as pltpu

F32 = jnp.float32
BF16 = jnp.bfloat16
HIGHEST = lax.Precision.HIGHEST

D_MODEL = 1024
DEPTH = 2
GRID_W = 64
HEADS = 4
HG_DK = 64
GLA_DK = 32
REC_DV = 64
REC_W = HEADS * REC_DV
MLA_NOPE = 128
MLA_ROPE = 64
MLA_DV = 128
MLA_Q_RANK = 256
MLA_KV_RANK = 128
MLA_QK = MLA_NOPE + MLA_ROPE
MLA_W = HEADS * MLA_DV
MLA_SCALE = MLA_QK ** -0.5
GLA_GATE_RANK = 16
GLA_TAU = 16.0
ROPE_BASE = 10000.0
ROPE_FREQS = MLA_ROPE // 4
NORM_EPS = 1e-6
FORGET_MIN = 1e-6
DEEPNORM_ALPHA = (2 * DEPTH) ** 0.25

CHUNK = 64
LANE = 128
SUBLANE = 8
MXU_DEPTH = 256
SUB = SUBLANE
NSUB = CHUNK // SUB
REC_BB = 8
ROW_TILE = 256
ATT_Q_GROUP = 256
ATT_Q_TILE = 4 * ATT_Q_GROUP
LOG2E = 1.4426950408889634
V_ROWS = MLA_DV + 16
HG_COLS = 5 * REC_W
MLA_COLS = 1024
MISC_COL = MLA_Q_RANK + MLA_KV_RANK
GLA_A_LANE = MLA_ROPE
GLA_PROJ_COLS = 768
GLA_COLS = GLA_PROJ_COLS + LANE
IN_COLS = HG_COLS + MLA_COLS + GLA_PROJ_COLS
MLA_GATE_COL = MLA_COLS - MLA_W

VMEM_LIMIT = 56 * 1024 * 1024


def _cparams(sem):
    return pltpu.CompilerParams(dimension_semantics=sem, vmem_limit_bytes=VMEM_LIMIT)


def _dot(a, b):
    return jnp.dot(a, b, preferred_element_type=F32)


def _dot_nt(a, b):
    return lax.dot_general(a, b, (((1,), (1,)), ((), ())), preferred_element_type=F32)


def _dot_tn(a, b):
    return lax.dot_general(a, b, (((0,), (0,)), ((), ())), preferred_element_type=F32)


def _silu(x):
    return x * jax.nn.sigmoid(x)


def _iota_div(shape, axis, div):
    assert div & (div - 1) == 0
    return lax.shift_right_logical(lax.broadcasted_iota(jnp.int32, shape, axis),
                                   jnp.int32(div.bit_length() - 1))


def _mod_kernel(c_ref, w_ref, b_ref, o_ref):
    s = _silu(c_ref[...])
    o_ref[0] = jnp.dot(s, w_ref[0], precision=HIGHEST, preferred_element_type=F32) + b_ref[0]


def _modulation(c_rows, w_mod, b_mod):
    rows = c_rows.shape[0]
    nt = 3 * D_MODEL // 1024
    return pl.pallas_call(
        _mod_kernel,
        grid=(DEPTH, nt),
        in_specs=[pl.BlockSpec((rows, D_MODEL), lambda l, j: (0, 0)),
                  pl.BlockSpec((1, D_MODEL, 1024), lambda l, j: (l, 0, j)),
                  pl.BlockSpec((1, 1, 1024), lambda l, j: (l, 0, j))],
        out_specs=pl.BlockSpec((1, rows, 1024), lambda l, j: (l, 0, j)),
        out_shape=jax.ShapeDtypeStruct((DEPTH, rows, 3 * D_MODEL), F32),
        compiler_params=_cparams(("arbitrary", "arbitrary")),
        name="modulation",
    )(c_rows, w_mod, b_mod.reshape(DEPTH, 1, 3 * D_MODEL))


def _layer_norm(x):
    mu = jnp.mean(x, axis=-1, keepdims=True)
    xc = x - mu
    var = jnp.mean(xc * xc, axis=-1, keepdims=True)
    return xc * lax.rsqrt(var + NORM_EPS)


def _rms_norm(x, g):
    return x * lax.rsqrt(jnp.mean(x * x, axis=-1, keepdims=True) + NORM_EPS) * g


def _swap_halves(x):
    n = x.shape[-1]
    lane = lax.broadcasted_iota(jnp.int32, x.shape, x.ndim - 1)
    first_half = (lane & (2 * ROPE_FREQS - 1)) < ROPE_FREQS
    return jnp.where(first_half, pltpu.roll(x, n - ROPE_FREQS, x.ndim - 1),
                     pltpu.roll(x, ROPE_FREQS, x.ndim - 1))


def _front_stage(is_ctx, x_ref, xc_ref, shift_ref, scale_ref, w_ref, qg_ref, kvg_ref, wuq_ref, wuk_ref, wvt_ref,
                 cos_ref, sin_ref, zhg_ref, zgla_ref, mgate_ref, q_ref, k_ref, vt_ref, h_w, h_r):
    x = jnp.where(is_ctx, xc_ref[0], x_ref[0])
    h_w[...] = (_layer_norm(x) * (1.0 + scale_ref[0]) + shift_ref[0]).astype(BF16)
    hb = h_r[...]
    z = _dot(hb, w_ref[:, HG_COLS:HG_COLS + MLA_COLS])
    zhg_ref[0] = _dot(hb, w_ref[:, 0:HG_COLS]).astype(BF16)
    mgate_ref[0] = z[:, MLA_GATE_COL:].astype(BF16)

    cq = _rms_norm(z[:, 0:MLA_Q_RANK], qg_ref[...]).astype(BF16)
    ckv = _rms_norm(z[:, MLA_Q_RANK:MLA_Q_RANK + MLA_KV_RANK], kvg_ref[...]).astype(BF16)
    kr = z[:, MISC_COL:MISC_COL + LANE]
    zgla_ref[0, :, GLA_PROJ_COLS:GLA_COLS] = kr.astype(BF16)
    qf = _dot(cq, wuq_ref[...]) * (MLA_SCALE * LOG2E)
    kf = _dot(ckv, wuk_ref[...])
    vt = _dot_nt(wvt_ref[...], ckv)
    zgla_ref[0, :, 0:GLA_PROJ_COLS] = _dot(hb, w_ref[:, HG_COLS + MLA_COLS:IN_COLS]).astype(BF16)
    cos = cos_ref[...]
    sin = sin_ref[...]
    q_rope = qf[:, HEADS * MLA_NOPE:]
    q_rope = (q_rope * jnp.concatenate([cos, cos], axis=-1)
              + _swap_halves(q_rope) * jnp.concatenate([sin, sin], axis=-1))
    kr = (kr * cos + _swap_halves(kr) * sin)[:, 0:MLA_ROPE]
    ones = jnp.ones((V_ROWS - MLA_DV, vt.shape[1]), BF16)
    for h in range(HEADS):
        q_ref[0, h, :, 0:MLA_NOPE] = qf[:, h * MLA_NOPE:(h + 1) * MLA_NOPE].astype(BF16)
        q_ref[0, h, :, MLA_NOPE:MLA_QK] = q_rope[:, h * MLA_ROPE:(h + 1) * MLA_ROPE].astype(BF16)
        k_ref[0, h, :, 0:MLA_NOPE] = kf[:, h * MLA_NOPE:(h + 1) * MLA_NOPE].astype(BF16)
        k_ref[0, h, :, MLA_NOPE:MLA_QK] = kr.astype(BF16)
        vt_ref[0, h, 0:MLA_DV, :] = vt[h * MLA_DV:(h + 1) * MLA_DV, :].astype(BF16)
        vt_ref[0, h, MLA_DV:V_ROWS, :] = ones


def _front_kernel(*refs, n_tiles, tiles_per_row, n_ctx_tiles):
    io, (h_a, h_b) = refs[:-2], refs[-2:]
    t = pl.program_id(0)
    is_ctx = (jnp.minimum(t, n_tiles - 1) % tiles_per_row) < n_ctx_tiles

    @pl.when(t == 0)
    def _():
        h_b[...] = jnp.zeros_like(h_b)

    @pl.when(t % 2 == 0)
    def _():
        _front_stage(is_ctx, *io, h_a, h_b)

    @pl.when(t % 2 == 1)
    def _():
        _front_stage(is_ctx, *io, h_b, h_a)


def _front(x, xc, mod, w_in_packed, q_norm_g, kv_norm_g, wuq_packed, wuk_packed, wvt_packed, cos_t, sin_t):
    B, N, _ = x.shape
    Lc = xc.shape[1]
    T = Lc + N
    tm = ROW_TILE
    nt, nct = T // tm, Lc // tm
    n_tiles = B * nt
    full = lambda shape: pl.BlockSpec(shape, lambda t: (0,) * len(shape))

    def norm_tile(t):
        t = jnp.minimum(t, n_tiles - 1)
        return t // nt, t % nt

    def proj_tile(t):
        t = jnp.maximum(t - 1, 0)
        return t // nt, t % nt

    def x_map(t):
        b, i = norm_tile(t)
        return b, jnp.maximum(i - nct, 0), 0

    def xc_map(t):
        b, i = norm_tile(t)
        return b, jnp.minimum(i, nct - 1), 0

    def mod_map(col):
        def index(t):
            b, i = norm_tile(t)
            return jnp.where(i < nct, B, b), 0, col
        return index

    def table_map(t):
        return proj_tile(t)[1], 0

    def row_map(t):
        b, i = proj_tile(t)
        return b, i, 0

    def q_map(t):
        b, i = proj_tile(t)
        return b, 0, jnp.where(i < nct, N // tm + i, i - nct), 0

    def k_map(t):
        b, i = proj_tile(t)
        return b, 0, i, 0

    def vt_map(t):
        b, i = proj_tile(t)
        return b, 0, 0, i

    return pl.pallas_call(
        functools.partial(_front_kernel, n_tiles=n_tiles, tiles_per_row=nt, n_ctx_tiles=nct),
        grid=(n_tiles + 1,),
        in_specs=[pl.BlockSpec((1, tm, D_MODEL), x_map),
                  pl.BlockSpec((1, tm, D_MODEL), xc_map),
                  pl.BlockSpec((1, 1, D_MODEL), mod_map(0)),
                  pl.BlockSpec((1, 1, D_MODEL), mod_map(1)),
                  full((D_MODEL, IN_COLS)), full((1, MLA_Q_RANK)), full((1, MLA_KV_RANK)),
                  full(wuq_packed.shape), full(wuk_packed.shape), full(wvt_packed.shape),
                  pl.BlockSpec((tm, LANE), table_map),
                  pl.BlockSpec((tm, LANE), table_map)],
        out_specs=[pl.BlockSpec((1, tm, HG_COLS), row_map),
                   pl.BlockSpec((1, tm, GLA_COLS), row_map),
                   pl.BlockSpec((1, tm, MLA_W), row_map),
                   pl.BlockSpec((1, HEADS, tm, MLA_QK), q_map),
                   pl.BlockSpec((1, HEADS, tm, MLA_QK), k_map),
                   pl.BlockSpec((1, HEADS, V_ROWS, tm), vt_map)],
        out_shape=[jax.ShapeDtypeStruct((B, T, HG_COLS), BF16),
                   jax.ShapeDtypeStruct((B, T, GLA_COLS), BF16),
                   jax.ShapeDtypeStruct((B, T, MLA_W), BF16),
                   jax.ShapeDtypeStruct((B, HEADS, T, MLA_QK), BF16),
                   jax.ShapeDtypeStruct((B, HEADS, T, MLA_QK), BF16),
                   jax.ShapeDtypeStruct((B, HEADS, V_ROWS, T), BF16)],
        scratch_shapes=[pltpu.VMEM((tm, D_MODEL), BF16), pltpu.VMEM((tm, D_MODEL), BF16)],
        compiler_params=_cparams(("arbitrary",)),
        name="front",
    )(x, xc, mod, mod, w_in_packed, q_norm_g.reshape(1, -1), kv_norm_g.reshape(1, -1),
      wuq_packed, wuk_packed, wvt_packed, cos_t, sin_t)


def _block_prefix(x, rev):
    row = lax.broadcasted_iota(jnp.int32, (SUB, x.shape[1]), 0)
    blocks = []
    for I in range(NSUB):
        b = x[I * SUB:(I + 1) * SUB, :]
        step = 1
        while step < SUB:
            if rev:
                b = b + jnp.where(row < SUB - step, pltpu.roll(b, SUB - step, 0), 0.0)
            else:
                b = b + jnp.where(row >= step, pltpu.roll(b, step, 0), 0.0)
            step *= 2
        blocks.append(b)
    return jnp.concatenate(blocks, axis=0)


def _block_prefix_mxu(x, tri):
    hi = x.astype(BF16)
    r1 = x - hi.astype(F32)
    mid = r1.astype(BF16)
    lo = (r1 - mid.astype(F32)).astype(BF16)
    return _dot(tri, hi) + _dot(tri, mid) + _dot(tri, lo)


def _rec_chain(prep, store_out, st_ref, head_sum, tri, rev, dk, min_step_log2):
    dkt = HEADS * dk
    inputs = prep()
    q, k, g, v = (yield from inputs) if inspect.isgenerator(inputs) else inputs
    if tri is None:
        lcum = _block_prefix(g, rev)
    else:
        lcum = _block_prefix_mxu(g, tri)
        yield

    phys = [(NSUB - 1 - p) if rev else p for p in range(NSUB)]
    rows = lambda I: slice(I * SUB, (I + 1) * SUB)
    last_row = [(I * SUB) if rev else (I * SUB + SUB - 1) for I in phys]
    ltot = [jnp.broadcast_to(lcum[r:r + 1, :], (SUB, dkt)) for r in last_row]

    q_in, k_out = [], []
    for p, I in enumerate(phys):
        lc = lcum[rows(I), :]
        q_in.append(q[rows(I), :] * jnp.exp2(lc))
        k_out.append(k[rows(I), :] * jnp.exp2(ltot[p] - lc))

    def running(order):
        sums, acc = {}, None
        for p in order:
            sums[p] = acc
            acc = ltot[p] if acc is None else acc + ltot[p]
        return sums, acc

    pre, total = running(range(NSUB))
    suf, _ = running(reversed(range(NSUB)))
    scaled = lambda x, e: x if e is None else x * jnp.exp2(e)
    q_hat = [scaled(q_in[p], pre[p]) for p in range(NSUB)]
    k_hat = [scaled(k_out[p], suf[p]) for p in range(NSUB)]

    def phys_concat(parts):
        order = sorted(range(NSUB), key=lambda p: phys[p])
        return jnp.concatenate([parts[p] for p in order], axis=0)

    lhs, lhs_at = [], {}
    for p in range(1, NSUB):
        gap = None
        for dist in range(1, p + 1):
            lhs_at[(p, dist)] = len(lhs)
            lhs.append(scaled(q_in[p], gap))
            gap = ltot[p - dist] if gap is None else gap + ltot[p - dist]
    k_out_all = phys_concat(k_out).astype(BF16)
    rk = _iota_div((HEADS * CHUNK, dkt), 0, CHUNK)
    ck = _iota_div((HEADS * CHUNK, dkt), 1, dk)
    kbd_t = jnp.where(rk == ck, jnp.concatenate([k_out_all] * HEADS, axis=0), 0)
    sc = _dot_nt(jnp.concatenate(lhs, axis=0).astype(BF16), kbd_t)

    if min_step_log2 is None:
        ri = lax.broadcasted_iota(jnp.int32, (SUB, dkt), 0)
        units = []
        for I in range(NSUB):
            q_i = q[rows(I), :]
            l_i = lcum[rows(I), :]
            for jj in range(SUB):
                r = I * SUB + jj
                diff = l_i - lcum[r:r + 1, :]
                if jj != (SUB - 1 if rev else 0):
                    diff = jnp.where((ri <= jj) if rev else (ri >= jj), diff, -1e30)
                units.append(q_i * k[r:r + 1, :] * jnp.exp2(diff))
        pack = head_sum.shape[0] // dkt
        packed = [jnp.concatenate(units[n:n + pack], axis=1) for n in range(0, len(units), pack)]
        w_all = _dot(jnp.concatenate(packed, axis=0).astype(BF16), head_sum)
    else:
        assert (SUB // 2) * -min_step_log2 < 120
        q_mid, k_mid = [], []
        for p, I in enumerate(phys):
            m = I * SUB + (SUB // 2 if rev else SUB // 2 - 1)
            rel = lcum[rows(I), :] - jnp.broadcast_to(lcum[m:m + 1, :], (SUB, dkt))
            q_mid.append(q[rows(I), :] * jnp.exp2(rel))
            k_mid.append(k[rows(I), :] * jnp.exp2(-rel))
        k_mid_all = phys_concat(k_mid).astype(BF16)
        kbd_mid_t = jnp.where(rk == ck, jnp.concatenate([k_mid_all] * HEADS, axis=0), 0)
        sc_diag = _dot_nt(phys_concat(q_mid).astype(BF16), kbd_mid_t)

    st = st_ref[...]
    vb = v.astype(BF16)
    rs = _iota_div((REC_W, dkt), 0, REC_DV)
    cs = _iota_div((REC_W, dkt), 1, dk)
    o_inter = _dot_nt(phys_concat(q_hat).astype(BF16), jnp.where(rs == cs, st.astype(BF16), 0))
    upd = _dot_tn(vb, phys_concat(k_hat).astype(BF16))
    yield

    lane = lax.broadcasted_iota(jnp.int32, (SUB, HEADS * CHUNK), 1)
    col_blk = lax.shift_right_logical(lane, jnp.int32(SUB.bit_length() - 1)) & (NSUB - 1)
    col_p = (NSUB - 1 - col_blk) if rev else col_blk
    key_pos = lane & (SUB - 1)
    row_pos = lax.broadcasted_iota(jnp.int32, (SUB, HEADS * CHUNK), 0)
    causal = (row_pos <= key_pos) if rev else (row_pos >= key_pos)
    p_rows = []
    for p, I in enumerate(phys):
        if min_step_log2 is None:
            groups = SUB // pack
            level = [w_all[(I * groups + n) * SUB:(I * groups + n + 1) * SUB, :] for n in range(groups)]
            bit = pack
            while len(level) > 1:
                take_odd = (lane & bit) != 0
                level = [jnp.where(take_odd, level[2 * n + 1], level[2 * n]) for n in range(len(level) // 2)]
                bit *= 2
            diag = level[0]
        else:
            diag = jnp.where(causal, sc_diag[rows(I), :], 0.0)
        acc_p = jnp.where(col_p == p, diag, 0.0)
        for dist in range(1, p + 1):
            n = lhs_at[(p, dist)]
            acc_p = jnp.where(col_p == p - dist, sc[n * SUB:(n + 1) * SUB, :], acc_p)
        p_rows.append(acc_p)
    p_all = phys_concat(p_rows).astype(BF16)
    rv = _iota_div((HEADS * CHUNK, REC_W), 0, CHUNK)
    cv = _iota_div((HEADS * CHUNK, REC_W), 1, REC_DV)
    vbd = jnp.where(rv == cv, jnp.concatenate([vb] * HEADS, axis=0), 0)
    o_intra = _dot(p_all, vbd)
    yield

    store_out(o_intra + o_inter)
    st_ref[...] = st * jnp.exp2(total[0:1, :]) + upd


def _run_interleaved(chains, skew):
    chains = list(chains)
    done = [False] * len(chains)
    started = 0
    while not all(done):
        started = min(started + skew, len(chains))
        for c in range(started):
            if not done[c]:
                try:
                    next(chains[c])
                except StopIteration:
                    done[c] = True


def _store_row(o_ref, i, o):
    o_ref[i] = o.astype(o_ref.dtype)


def _hg_prep(z_ref, i, lb_ref, d):
    col = lambda n: z_ref[i, :, n * REC_W:(n + 1) * REC_W].astype(F32)
    lb = lb_ref[d:d + 1, :]
    q = _silu(col(0))
    v = col(3)
    zz = col(1 + d)
    f = lb + (1.0 - lb) * jax.nn.sigmoid(zz)
    g = jnp.log2(jnp.maximum(f, FORGET_MIN))
    k = (1.0 - lb) * jax.nn.sigmoid(-zz)
    return q, k, g, v


def _log_sigmoid(x):
    return jnp.minimum(x, 0.0) - jnp.log(1.0 + jnp.exp(-jnp.abs(x)))


def _gla_prep(z_ref, i, wg_ref, bg_ref, d):
    dkt = HEADS * GLA_DK
    q = z_ref[i, :, 0:dkt].astype(F32) * (GLA_DK ** -0.5)
    k = z_ref[i, :, dkt:2 * dkt].astype(F32)
    v = z_ref[i, :, 2 * dkt:2 * dkt + REC_W].astype(F32)
    logits = _dot(z_ref[i, :, GLA_COLS - LANE:GLA_COLS], wg_ref[d])
    yield
    g = _log_sigmoid(logits + bg_ref[d]) * (LOG2E / GLA_TAU)
    return q, k, g, v


def _rec_kernel(*refs, mixer, dk, bb):
    if mixer == "hg":
        zf_ref, zb_ref, lb_ref, hsum_ref, of_ref, ob_ref, stf, stb = refs
        tri_ref = None
    else:
        zf_ref, zb_ref, wg_ref, bg_ref, tri_ref, hsum_ref, of_ref, ob_ref, stf, stb = refs

    @pl.when(pl.program_id(1) == 0)
    def _():
        stf[...] = jnp.zeros_like(stf)
        stb[...] = jnp.zeros_like(stb)

    head_sum = hsum_ref[...]
    min_step_log2 = float(np.log2(FORGET_MIN)) if mixer == "hg" else None
    chains = []
    for i in range(bb):
        for d, (z_ref, o_ref, st) in enumerate(((zf_ref, of_ref, stf), (zb_ref, ob_ref, stb))):
            if mixer == "hg":
                prep = functools.partial(_hg_prep, z_ref, i, lb_ref, d)
            else:
                prep = functools.partial(_gla_prep, z_ref, i, wg_ref, bg_ref, d)
            chains.append(_rec_chain(prep, functools.partial(_store_row, o_ref, i), st.at[i], head_sum,
                                     None if tri_ref is None else tri_ref[d],
                                     rev=(d == 1), dk=dk, min_step_log2=min_step_log2))
    _run_interleaved(chains, skew=1 if mixer == "hg" else len(chains))


def _head_sum_weight(dk):
    dkt = HEADS * dk
    pack = max(1, MXU_DEPTH // dkt)
    w = np.zeros((pack * dkt, HEADS * CHUNK), np.float32)
    for u in range(pack):
        for h in range(HEADS):
            for j in range(u, CHUNK, pack):
                w[u * dkt + h * dk:u * dkt + (h + 1) * dk, h * CHUNK + j] = 1.0
    return jnp.asarray(w, BF16)


def _recurrence(z, mixer, params, n_ctx):
    B, T, cols = z.shape
    dk = HG_DK if mixer == "hg" else GLA_DK
    dkt = HEADS * dk
    nc, ncc = T // CHUNK, n_ctx // CHUNK
    bb = REC_BB if B % REC_BB == 0 else 1
    head_sum = _head_sum_weight(dk)
    full = lambda shape: pl.BlockSpec(shape, lambda b, c: (0,) * len(shape))
    fwd = lambda b, c: (b, c, 0)
    bwd = lambda b, c: (b, jnp.where(c < ncc, ncc - 1 - c, nc + ncc - 1 - c), 0)
    in_specs = [pl.BlockSpec((bb, CHUNK, cols), fwd), pl.BlockSpec((bb, CHUNK, cols), bwd)]
    in_specs += [full(p.shape) for p in params]
    in_specs += [full(head_sum.shape)]
    return pl.pallas_call(
        functools.partial(_rec_kernel, mixer=mixer, dk=dk, bb=bb),
        grid=(B // bb, nc),
        in_specs=in_specs,
        out_specs=[pl.BlockSpec((bb, CHUNK, REC_W), fwd), pl.BlockSpec((bb, CHUNK, REC_W), bwd)],
        out_shape=[jax.ShapeDtypeStruct((B, T, REC_W), BF16)] * 2,
        scratch_shapes=[pltpu.VMEM((bb, REC_W, dkt), F32), pltpu.VMEM((bb, REC_W, dkt), F32)],
        compiler_params=_cparams(("arbitrary", "arbitrary")),
        name="recurrence_" + mixer,
    )(z, z, *params, head_sum)


def _attn_stage(q_ref, k_ref, vt_ref, o_ref, s_w, m_w, s_r, m_r):
    n_keys = k_ref.shape[2]
    half = n_keys // 2
    parts = [slice(0, half), slice(half, n_keys)]
    groups = [slice(j, j + ATT_Q_GROUP) for j in range(0, q_ref.shape[2], ATT_Q_GROUP)]
    for qs in groups:
        q = q_ref[0, 0, qs, :]
        m = None
        for ks in parts:
            s = _dot_nt(k_ref[0, 0, ks, :], q)
            s_w[ks, qs] = s
            mx = jnp.max(s, axis=0, keepdims=True)
            m = mx if m is None else jnp.maximum(m, mx)
        m_w[0:1, qs] = m
        m_prev = m_r[0:1, qs]
        acc = None
        for ks in parts:
            part = _dot(vt_ref[0, 0, :, ks], jnp.exp2(s_r[ks, qs] - m_prev).astype(BF16))
            acc = part if acc is None else acc + part
        o_t = acc[0:MLA_DV, :] * (1.0 / acc[MLA_DV:MLA_DV + 1, :])
        o_ref[0, qs, :] = o_t.T.astype(o_ref.dtype)


def _attn_kernel(q_ref, k_ref, vt_ref, o_ref, s_a, m_a, s_b, m_b):
    t = pl.program_id(0)

    @pl.when(t == 0)
    def _():
        s_b[...] = jnp.zeros_like(s_b)
        m_b[...] = jnp.zeros_like(m_b)

    @pl.when(t % 2 == 0)
    def _():
        _attn_stage(q_ref, k_ref, vt_ref, o_ref, s_a, m_a, s_b, m_b)

    @pl.when(t % 2 == 1)
    def _():
        _attn_stage(q_ref, k_ref, vt_ref, o_ref, s_b, m_b, s_a, m_a)


def _attention(q, k, vt, row0, n_rows, n_keys):
    B, H, _, _ = q.shape
    tq = ATT_Q_TILE if (n_rows % ATT_Q_TILE == 0 and row0 % ATT_Q_TILE == 0) else ATT_Q_GROUP
    assert n_keys % (2 * LANE) == 0 and row0 % tq == 0 and n_rows % tq == 0
    nq, q0 = n_rows // tq, row0 // tq
    n_tiles = B * H * nq

    def tile(t):
        return t // (H * nq), (t // nq) % H, t % nq

    def score_tile(t):
        return tile(jnp.minimum(t, n_tiles - 1))

    def finish_tile(t):
        return tile(jnp.maximum(t - 1, 0))

    def q_map(t):
        b, h, i = score_tile(t)
        return b, h, q0 + i, 0

    def k_map(t):
        b, h, _ = score_tile(t)
        return b, h, 0, 0

    def vt_map(t):
        b, h, _ = finish_tile(t)
        return b, h, 0, 0

    def o_map(t):
        b, h, i = finish_tile(t)
        return b, i, h

    return pl.pallas_call(
        _attn_kernel,
        grid=(n_tiles + 1,),
        in_specs=[pl.BlockSpec((1, 1, tq, MLA_QK), q_map),
                  pl.BlockSpec((1, 1, n_keys, MLA_QK), k_map),
                  pl.BlockSpec((1, 1, V_ROWS, n_keys), vt_map)],
        out_specs=pl.BlockSpec((1, tq, MLA_DV), o_map),
        out_shape=jax.ShapeDtypeStruct((B, n_rows, H * MLA_DV), BF16),
        scratch_shapes=[pltpu.VMEM((n_keys, tq), F32), pltpu.VMEM((SUBLANE, tq), F32),
                        pltpu.VMEM((n_keys, tq), F32), pltpu.VMEM((SUBLANE, tq), F32)],
        compiler_params=_cparams(("arbitrary",)),
        name="attention",
    )(q, k, vt)


def _head_rms(o, ones_bf, g):
    ms = _dot((o * o).astype(BF16), ones_bf) * (1.0 / REC_DV)
    return o * lax.rsqrt(ms + NORM_EPS) * g


def _out_kernel(*refs, n_ctx_tiles):
    if n_ctx_tiles:
        x_ref, att_ref, xc_ref, attc_ref = refs[0:4]
        rest = refs[4:]
    else:
        x_ref, att_ref = refs[0:2]
        rest = refs[2:]
    (gate_ref, zhg_ref, zmla_ref, zgla_ref, hgf_ref, hgb_ref, glf_ref, glb_ref,
     ones_ref, hgn_ref, gln_ref, w_ref, lng_ref, lnb_ref) = rest[0:14]
    x, att = x_ref[0], att_ref[0]
    if n_ctx_tiles:
        is_ctx = pl.program_id(1) < n_ctx_tiles
        x = jnp.where(is_ctx, xc_ref[0], x)
        att = jnp.where(is_ctx, attc_ref[0], att)
    f32 = lambda ref: ref[0].astype(F32)
    ones_bf = ones_ref[...]
    y_hg = _head_rms(f32(hgf_ref) + f32(hgb_ref), ones_bf, hgn_ref[...]) * _silu(f32(zhg_ref))
    y_mla = att.astype(F32) * _silu(f32(zmla_ref))
    y_gla = _head_rms(f32(glf_ref) + f32(glb_ref), ones_bf, gln_ref[...]) * _silu(f32(zgla_ref))
    proj = (_dot(y_hg.astype(BF16), w_ref[0:REC_W, :])
            + _dot(y_mla.astype(BF16), w_ref[REC_W:REC_W + MLA_W, :])
            + _dot(y_gla.astype(BF16), w_ref[REC_W + MLA_W:, :]))
    r = DEEPNORM_ALPHA * x + gate_ref[0] * proj
    res = _layer_norm(r) * lng_ref[...] + lnb_ref[...]
    o_ref = rest[14]
    o_ref[0] = res
    if n_ctx_tiles:
        oc_ref = rest[15]

        @pl.when(is_ctx)
        def _():
            oc_ref[0] = res


def _output_block(x, att, xc, att_c, mod, z_hg, mla_gate, z_gla, hg_f, hg_b, gl_f, gl_b,
                  hg_norm_g, gla_norm_g, w_out_bf, ln_g, ln_b):
    B, N, _ = x.shape
    T = z_hg.shape[1]
    tm = ROW_TILE
    with_ctx = xc is not None
    nct = (T - N) // tm
    first = 0 if with_ctx else nct
    lat = lambda i: jnp.maximum(i + first - nct, 0)
    ones = np.zeros((REC_W, REC_W), np.float32)
    for r in range(REC_W):
        ones[r, (r // REC_DV) * REC_DV:(r // REC_DV + 1) * REC_DV] = 1.0
    ones = jnp.asarray(ones, BF16)
    row = lambda w, col=0: pl.BlockSpec((1, tm, w), lambda b, i: (b, i + first, col))
    lat_row = lambda w: pl.BlockSpec((1, tm, w), lambda b, i: (b, lat(i), 0))
    ctx_row = lambda w: pl.BlockSpec((1, tm, w), lambda b, i: (b, jnp.minimum(i, nct - 1), 0))
    full = lambda shape: pl.BlockSpec(shape, lambda b, i: (0,) * len(shape))
    mod_row = (lambda b, i: jnp.where(i < nct, B, b)) if with_ctx else (lambda b, i: b)
    in_specs = [lat_row(D_MODEL), lat_row(MLA_W)]
    args = [x, att]
    if with_ctx:
        in_specs += [ctx_row(D_MODEL), ctx_row(MLA_W)]
        args += [xc, att_c]
    in_specs += [pl.BlockSpec((1, 1, D_MODEL), lambda b, i: (mod_row(b, i), 0, 2)),
                 row(REC_W, HG_COLS // REC_W - 1),
                 row(MLA_W),
                 row(REC_W, 2),
                 row(REC_W), row(REC_W), row(REC_W), row(REC_W),
                 full((REC_W, REC_W)), full((1, REC_W)), full((1, REC_W)),
                 full((D_MODEL, D_MODEL)), full((1, D_MODEL)), full((1, D_MODEL))]
    args += [mod, z_hg, mla_gate, z_gla, hg_f, hg_b, gl_f, gl_b, ones,
             jnp.tile(hg_norm_g, HEADS).reshape(1, REC_W), jnp.tile(gla_norm_g, HEADS).reshape(1, REC_W),
             w_out_bf, ln_g.reshape(1, D_MODEL), ln_b.reshape(1, D_MODEL)]
    out_specs = [lat_row(D_MODEL)]
    out_shape = [jax.ShapeDtypeStruct((B, N, D_MODEL), F32)]
    if with_ctx:
        out_specs.append(ctx_row(D_MODEL))
        out_shape.append(jax.ShapeDtypeStruct(xc.shape, F32))
    outs = pl.pallas_call(
        functools.partial(_out_kernel, n_ctx_tiles=nct if with_ctx else 0),
        grid=(B, T // tm - first),
        in_specs=in_specs,
        out_specs=out_specs,
        out_shape=out_shape,
        compiler_params=_cparams(("arbitrary", "arbitrary")),
        name="output_block",
    )(*args)
    return (outs[0], outs[1]) if with_ctx else (outs[0], None)


def _pack_w_in(w):
    o = np.cumsum([0, 256, 256, 256, 256, 256, 256, 128, 64, 512, 128, 128, 256, 16, 16, 256])
    col = lambda i: w[:, int(o[i]):int(o[i + 1])]
    zeros = lambda n: jnp.zeros((w.shape[0], n), w.dtype)
    hg = [col(0), col(1), col(2), col(3), col(4)]
    mla = [col(5), col(6), col(7), col(12), col(13), zeros(LANE - MLA_ROPE - 2 * GLA_GATE_RANK), col(8)]
    gla = [col(9), col(10), col(11), col(14)]
    return jnp.concatenate(hg + mla + gla, axis=1).astype(BF16)


def _pack_wuq(w):
    w4 = w.reshape(MLA_Q_RANK, HEADS, MLA_QK)
    return jnp.concatenate([w4[:, :, :MLA_NOPE].reshape(MLA_Q_RANK, -1),
                            w4[:, :, MLA_NOPE:].reshape(MLA_Q_RANK, -1)], axis=1).astype(BF16)


def _pack_wukv(w):
    w4 = w.reshape(MLA_KV_RANK, HEADS, MLA_NOPE + MLA_DV)
    wk = w4[:, :, :MLA_NOPE].reshape(MLA_KV_RANK, -1).astype(BF16)
    wvt = w4[:, :, MLA_NOPE:].reshape(MLA_KV_RANK, -1).T.astype(BF16)
    return wk, wvt


def _pack_gla_gate(w_a2, b_a):
    dkt = HEADS * GLA_DK
    wg = jnp.zeros((2, LANE, dkt), F32)
    wg = wg.at[0, GLA_A_LANE:GLA_A_LANE + GLA_GATE_RANK].set(w_a2[0])
    wg = wg.at[1, GLA_A_LANE + GLA_GATE_RANK:GLA_A_LANE + 2 * GLA_GATE_RANK].set(w_a2[1])
    return wg.astype(BF16), b_a.reshape(2, 1, dkt)


def _prefix_matrices():
    i = np.arange(CHUNK)
    same = (i[:, None] // SUB) == (i[None, :] // SUB)
    return jnp.asarray(np.stack([same & (i[None, :] <= i[:, None]), same & (i[None, :] >= i[:, None])]), BF16)


def _rope_tables(n_ctx, n):
    rows = n // GRID_W
    pos_r = jnp.repeat(jnp.arange(rows, dtype=F32), GRID_W)
    pos_c = jnp.tile(jnp.arange(GRID_W, dtype=F32), rows)
    inv = 1.0 / (ROPE_BASE ** (jnp.arange(ROPE_FREQS, dtype=F32) / ROPE_FREQS))
    ang = jnp.stack([pos_r, pos_c], axis=-1)[:, :, None] * inv
    cos = jnp.cos(ang)
    sin = jnp.sin(ang)
    cos64 = jnp.concatenate([cos, cos], axis=-1).reshape(n, MLA_ROPE)
    sin64 = jnp.concatenate([-sin, sin], axis=-1).reshape(n, MLA_ROPE)
    cos_t = jnp.tile(cos64, (1, LANE // MLA_ROPE))
    sin_t = jnp.tile(sin64, (1, LANE // MLA_ROPE))
    return (jnp.concatenate([jnp.ones((n_ctx, LANE), F32), cos_t], axis=0),
            jnp.concatenate([jnp.zeros((n_ctx, LANE), F32), sin_t], axis=0))


def kernel(x, c, ctx, c_ctx, w_mod, b_mod, w_in, w_out, ln_g, ln_b, hg_lb_logits, hg_norm_g,
           mla_q_norm_g, mla_kv_norm_g, mla_w_uq, mla_w_ukv, gla_w_a2, gla_b_a, gla_norm_g):
    B, N, _ = x.shape
    Lc = ctx.shape[1]
    assert N % ROW_TILE == 0 and Lc % ROW_TILE == 0 and N % GRID_W == 0
    cos_t, sin_t = _rope_tables(Lc, N)

    lb_soft = jax.nn.softmax(hg_lb_logits.astype(F32), axis=0)
    hg_lb = jnp.clip(jnp.cumsum(lb_soft, axis=0) - lb_soft[0:1], 0.0, 1.0)

    n_rows = -(-(B + 1) // SUBLANE) * SUBLANE
    c_rows = jnp.zeros((n_rows, D_MODEL), F32).at[0:B].set(c).at[B].set(c_ctx)
    mods = _modulation(c_rows, w_mod, b_mod)

    xc = ctx
    for l in range(DEPTH):
        need_ctx = l < DEPTH - 1
        mod = mods[l].reshape(n_rows, 1, 3 * D_MODEL)
        wuk_p, wvt_p = _pack_wukv(mla_w_ukv[l])
        z_hg, z_gla, mla_gate, q, k, vt = _front(
            x, xc, mod, _pack_w_in(w_in[l]), mla_q_norm_g[l], mla_kv_norm_g[l],
            _pack_wuq(mla_w_uq[l]), wuk_p, wvt_p, cos_t, sin_t)

        hg_f, hg_b = _recurrence(z_hg, "hg", (hg_lb[l],), Lc)
        gl_f, gl_b = _recurrence(z_gla, "gla", _pack_gla_gate(gla_w_a2[l], gla_b_a[l]) + (_prefix_matrices(),), Lc)
        att = _attention(q, k, vt, 0, N, Lc + N)
        att_c = _attention(q, k, vt, N, Lc, Lc) if need_ctx else None

        x, xc_new = _output_block(x, att, xc if need_ctx else None, att_c, mod, z_hg, mla_gate, z_gla,
                                  hg_f, hg_b, gl_f, gl_b, hg_norm_g[l], gla_norm_g[l],
                                  w_out[l].astype(BF16), ln_g[l], ln_b[l])
        if need_ctx:
            xc = xc_new
    return x
```

```python
import functools
import inspect

import numpy as np
import jax
import jax.numpy as jnp
from jax import lax
from jax.experimental import pallas as pl
from jax.experimental.pallas import tpu as pltpu

F32 = jnp.float32
BF16 = jnp.bfloat16
HIGHEST = lax.Precision.HIGHEST

D_MODEL = 1024
DEPTH = 2
GRID_W = 64
HEADS = 4
HG_DK = 64
GLA_DK = 32
REC_DV = 64
REC_W = HEADS * REC_DV
MLA_NOPE = 128
MLA_ROPE = 64
MLA_DV = 128
MLA_Q_RANK = 256
MLA_KV_RANK = 128
MLA_QK = MLA_NOPE + MLA_ROPE
MLA_W = HEADS * MLA_DV
MLA_SCALE = MLA_QK ** -0.5
GLA_GATE_RANK = 16
GLA_TAU = 16.0
ROPE_BASE = 10000.0
ROPE_FREQS = MLA_ROPE // 4
NORM_EPS = 1e-6
FORGET_MIN = 1e-6
DEEPNORM_ALPHA = (2 * DEPTH) ** 0.25

CHUNK = 64
LANE = 128
SUBLANE = 8
MXU_DEPTH = 256
SUB = SUBLANE
NSUB = CHUNK // SUB
REC_BB = 8
ROW_TILE = 256
ATT_Q_GROUP = 256
ATT_Q_TILE = 4 * ATT_Q_GROUP
LOG2E = 1.4426950408889634
GLA_FAST_MIN_LOG2 = -24.0
GLA_FAST_MIN_LOGIT = GLA_FAST_MIN_LOG2 * GLA_TAU / LOG2E + 1.0
V_ROWS = MLA_DV + 16
HG_COLS = 5 * REC_W
MLA_COLS = 1024
MISC_COL = MLA_Q_RANK + MLA_KV_RANK
GLA_A_LANE = MLA_ROPE
GLA_PROJ_COLS = 768
GLA_COLS = GLA_PROJ_COLS + LANE
IN_COLS = HG_COLS + MLA_COLS + GLA_PROJ_COLS
MLA_GATE_COL = MLA_COLS - MLA_W

VMEM_LIMIT = 56 * 1024 * 1024


def _cparams(sem):
    return pltpu.CompilerParams(dimension_semantics=sem, vmem_limit_bytes=VMEM_LIMIT)


def _dot(a, b):
    return jnp.dot(a, b, preferred_element_type=F32)


def _dot_nt(a, b):
    return lax.dot_general(a, b, (((1,), (1,)), ((), ())), preferred_element_type=F32)


def _dot_tn(a, b):
    return lax.dot_general(a, b, (((0,), (0,)), ((), ())), preferred_element_type=F32)


def _silu(x):
    return x * jax.nn.sigmoid(x)


def _iota_div(shape, axis, div):
    assert div & (div - 1) == 0
    return lax.shift_right_logical(lax.broadcasted_iota(jnp.int32, shape, axis),
                                   jnp.int32(div.bit_length() - 1))


def _mod_kernel(c_ref, w_ref, b_ref, o_ref):
    s = _silu(c_ref[...])
    o_ref[0] = jnp.dot(s, w_ref[0], precision=HIGHEST, preferred_element_type=F32) + b_ref[0]


def _modulation(c_rows, w_mod, b_mod):
    rows = c_rows.shape[0]
    nt = 3 * D_MODEL // 1024
    return pl.pallas_call(
        _mod_kernel,
        grid=(DEPTH, nt),
        in_specs=[pl.BlockSpec((rows, D_MODEL), lambda l, j: (0, 0)),
                  pl.BlockSpec((1, D_MODEL, 1024), lambda l, j: (l, 0, j)),
                  pl.BlockSpec((1, 1, 1024), lambda l, j: (l, 0, j))],
        out_specs=pl.BlockSpec((1, rows, 1024), lambda l, j: (l, 0, j)),
        out_shape=jax.ShapeDtypeStruct((DEPTH, rows, 3 * D_MODEL), F32),
        compiler_params=_cparams(("arbitrary", "arbitrary")),
        name="modulation",
    )(c_rows, w_mod, b_mod.reshape(DEPTH, 1, 3 * D_MODEL))


def _layer_norm(x):
    mu = jnp.mean(x, axis=-1, keepdims=True)
    xc = x - mu
    var = jnp.mean(xc * xc, axis=-1, keepdims=True)
    return xc * lax.rsqrt(var + NORM_EPS)


def _rms_norm(x, g):
    return x * lax.rsqrt(jnp.mean(x * x, axis=-1, keepdims=True) + NORM_EPS) * g


def _swap_halves(x):
    n = x.shape[-1]
    lane = lax.broadcasted_iota(jnp.int32, x.shape, x.ndim - 1)
    first_half = (lane & (2 * ROPE_FREQS - 1)) < ROPE_FREQS
    return jnp.where(first_half, pltpu.roll(x, n - ROPE_FREQS, x.ndim - 1),
                     pltpu.roll(x, ROPE_FREQS, x.ndim - 1))


def _front_kernel(x_ref, xc_ref, shift_ref, scale_ref, w_ref, qg_ref, kvg_ref, wuq_ref, wuk_ref, wvt_ref,
                  cos_ref, sin_ref, zhg_ref, zgla_ref, mgate_ref, q_ref, k_ref, vt_ref, *, n_ctx_tiles):
    is_ctx = pl.program_id(1) < n_ctx_tiles
    x = jnp.where(is_ctx, xc_ref[0], x_ref[0])
    hb = (_layer_norm(x) * (1.0 + scale_ref[0]) + shift_ref[0]).astype(BF16)
    z = _dot(hb, w_ref[:, HG_COLS:HG_COLS + MLA_COLS])
    zhg_ref[0] = _dot(hb, w_ref[:, 0:HG_COLS]).astype(BF16)
    mgate_ref[0] = z[:, MLA_GATE_COL:].astype(BF16)

    cq = _rms_norm(z[:, 0:MLA_Q_RANK], qg_ref[...]).astype(BF16)
    ckv = _rms_norm(z[:, MLA_Q_RANK:MLA_Q_RANK + MLA_KV_RANK], kvg_ref[...]).astype(BF16)
    kr = z[:, MISC_COL:MISC_COL + LANE]
    zgla_ref[0, :, GLA_PROJ_COLS:GLA_COLS] = kr.astype(BF16)
    qf = _dot(cq, wuq_ref[...]) * (MLA_SCALE * LOG2E)
    kf = _dot(ckv, wuk_ref[...])
    vt = _dot_nt(wvt_ref[...], ckv)
    zgla_ref[0, :, 0:GLA_PROJ_COLS] = _dot(hb, w_ref[:, HG_COLS + MLA_COLS:IN_COLS]).astype(BF16)
    cos = cos_ref[...]
    sin = sin_ref[...]
    q_rope = qf[:, HEADS * MLA_NOPE:]
    q_rope = (q_rope * jnp.concatenate([cos, cos], axis=-1)
              + _swap_halves(q_rope) * jnp.concatenate([sin, sin], axis=-1))
    kr = (kr * cos + _swap_halves(kr) * sin)[:, 0:MLA_ROPE]
    ones = jnp.ones((V_ROWS - MLA_DV, vt.shape[1]), BF16)
    for h in range(HEADS):
        q_ref[0, h, :, 0:MLA_NOPE] = qf[:, h * MLA_NOPE:(h + 1) * MLA_NOPE].astype(BF16)
        q_ref[0, h, :, MLA_NOPE:MLA_QK] = q_rope[:, h * MLA_ROPE:(h + 1) * MLA_ROPE].astype(BF16)
        k_ref[0, h, :, 0:MLA_NOPE] = kf[:, h * MLA_NOPE:(h + 1) * MLA_NOPE].astype(BF16)
        k_ref[0, h, :, MLA_NOPE:MLA_QK] = kr.astype(BF16)
        vt_ref[0, h, 0:MLA_DV, :] = vt[h * MLA_DV:(h + 1) * MLA_DV, :].astype(BF16)
        vt_ref[0, h, MLA_DV:V_ROWS, :] = ones


def _front(x, xc, mod, w_in_packed, q_norm_g, kv_norm_g, wuq_packed, wuk_packed, wvt_packed, cos_t, sin_t):
    B, N, _ = x.shape
    Lc = xc.shape[1]
    T = Lc + N
    tm = ROW_TILE
    nct = Lc // tm
    full = lambda shape: pl.BlockSpec(shape, lambda b, i: (0,) * len(shape))
    mod_row = lambda b, i: jnp.where(i < nct, B, b)
    return pl.pallas_call(
        functools.partial(_front_kernel, n_ctx_tiles=nct),
        grid=(B, T // tm),
        in_specs=[pl.BlockSpec((1, tm, D_MODEL), lambda b, i: (b, jnp.maximum(i - nct, 0), 0)),
                  pl.BlockSpec((1, tm, D_MODEL), lambda b, i: (b, jnp.minimum(i, nct - 1), 0)),
                  pl.BlockSpec((1, 1, D_MODEL), lambda b, i: (mod_row(b, i), 0, 0)),
                  pl.BlockSpec((1, 1, D_MODEL), lambda b, i: (mod_row(b, i), 0, 1)),
                  full((D_MODEL, IN_COLS)), full((1, MLA_Q_RANK)), full((1, MLA_KV_RANK)),
                  full(wuq_packed.shape), full(wuk_packed.shape), full(wvt_packed.shape),
                  pl.BlockSpec((tm, LANE), lambda b, i: (i, 0)),
                  pl.BlockSpec((tm, LANE), lambda b, i: (i, 0))],
        out_specs=[pl.BlockSpec((1, tm, HG_COLS), lambda b, i: (b, i, 0)),
                   pl.BlockSpec((1, tm, GLA_COLS), lambda b, i: (b, i, 0)),
                   pl.BlockSpec((1, tm, MLA_W), lambda b, i: (b, i, 0)),
                   pl.BlockSpec((1, HEADS, tm, MLA_QK),
                                lambda b, i: (b, 0, jnp.where(i < nct, N // tm + i, i - nct), 0)),
                   pl.BlockSpec((1, HEADS, tm, MLA_QK), lambda b, i: (b, 0, i, 0)),
                   pl.BlockSpec((1, HEADS, V_ROWS, tm), lambda b, i: (b, 0, 0, i))],
        out_shape=[jax.ShapeDtypeStruct((B, T, HG_COLS), BF16),
                   jax.ShapeDtypeStruct((B, T, GLA_COLS), BF16),
                   jax.ShapeDtypeStruct((B, T, MLA_W), BF16),
                   jax.ShapeDtypeStruct((B, HEADS, T, MLA_QK), BF16),
                   jax.ShapeDtypeStruct((B, HEADS, T, MLA_QK), BF16),
                   jax.ShapeDtypeStruct((B, HEADS, V_ROWS, T), BF16)],
        compiler_params=_cparams(("arbitrary", "arbitrary")),
        name="front",
    )(x, xc, mod, mod, w_in_packed, q_norm_g.reshape(1, -1), kv_norm_g.reshape(1, -1),
      wuq_packed, wuk_packed, wvt_packed, cos_t, sin_t)


def _block_prefix(x, rev):
    row = lax.broadcasted_iota(jnp.int32, (SUB, x.shape[1]), 0)
    blocks = []
    for I in range(NSUB):
        b = x[I * SUB:(I + 1) * SUB, :]
        step = 1
        while step < SUB:
            if rev:
                b = b + jnp.where(row < SUB - step, pltpu.roll(b, SUB - step, 0), 0.0)
            else:
                b = b + jnp.where(row >= step, pltpu.roll(b, step, 0), 0.0)
            step *= 2
        blocks.append(b)
    return jnp.concatenate(blocks, axis=0)


def _block_prefix_mxu(x, tri):
    hi = x.astype(BF16)
    r1 = x - hi.astype(F32)
    mid = r1.astype(BF16)
    lo = (r1 - mid.astype(F32)).astype(BF16)
    return _dot(tri, hi) + _dot(tri, mid) + _dot(tri, lo)


def _rec_chain(prep, store_out, st_ref, head_sum, tri, rev, dk, min_step_log2):
    dkt = HEADS * dk
    inputs = prep()
    q, k, g, v = (yield from inputs) if inspect.isgenerator(inputs) else inputs
    if tri is None:
        lcum = _block_prefix(g, rev)
    else:
        lcum = _block_prefix_mxu(g, tri)
        yield

    phys = [(NSUB - 1 - p) if rev else p for p in range(NSUB)]
    rows = lambda I: slice(I * SUB, (I + 1) * SUB)
    last_row = [(I * SUB) if rev else (I * SUB + SUB - 1) for I in phys]
    ltot = [jnp.broadcast_to(lcum[r:r + 1, :], (SUB, dkt)) for r in last_row]

    q_in, k_out = [], []
    for p, I in enumerate(phys):
        lc = lcum[rows(I), :]
        q_in.append(q[rows(I), :] * jnp.exp2(lc))
        k_out.append(k[rows(I), :] * jnp.exp2(ltot[p] - lc))

    def running(order):
        sums, acc = {}, None
        for p in order:
            sums[p] = acc
            acc = ltot[p] if acc is None else acc + ltot[p]
        return sums, acc

    pre, total = running(range(NSUB))
    suf, _ = running(reversed(range(NSUB)))
    scaled = lambda x, e: x if e is None else x * jnp.exp2(e)
    q_hat = [scaled(q_in[p], pre[p]) for p in range(NSUB)]
    k_hat = [scaled(k_out[p], suf[p]) for p in range(NSUB)]

    def phys_concat(parts):
        order = sorted(range(NSUB), key=lambda p: phys[p])
        return jnp.concatenate([parts[p] for p in order], axis=0)

    lhs, lhs_at = [], {}
    for p in range(1, NSUB):
        gap = None
        for dist in range(1, p + 1):
            lhs_at[(p, dist)] = len(lhs)
            lhs.append(scaled(q_in[p], gap))
            gap = ltot[p - dist] if gap is None else gap + ltot[p - dist]
    k_out_all = phys_concat(k_out).astype(BF16)
    rk = _iota_div((HEADS * CHUNK, dkt), 0, CHUNK)
    ck = _iota_div((HEADS * CHUNK, dkt), 1, dk)
    kbd_t = jnp.where(rk == ck, jnp.concatenate([k_out_all] * HEADS, axis=0), 0)
    sc = _dot_nt(jnp.concatenate(lhs, axis=0).astype(BF16), kbd_t)

    if min_step_log2 is None:
        ri = lax.broadcasted_iota(jnp.int32, (SUB, dkt), 0)
        units = []
        for I in range(NSUB):
            q_i = q[rows(I), :]
            l_i = lcum[rows(I), :]
            for jj in range(SUB):
                r = I * SUB + jj
                diff = l_i - lcum[r:r + 1, :]
                if jj != (SUB - 1 if rev else 0):
                    diff = jnp.where((ri <= jj) if rev else (ri >= jj), diff, -1e30)
                units.append(q_i * k[r:r + 1, :] * jnp.exp2(diff))
        pack = head_sum.shape[0] // dkt
        packed = [jnp.concatenate(units[n:n + pack], axis=1) for n in range(0, len(units), pack)]
        w_all = _dot(jnp.concatenate(packed, axis=0).astype(BF16), head_sum)
    else:
        assert (SUB // 2) * -min_step_log2 < 120
        q_mid, k_mid = [], []
        for p, I in enumerate(phys):
            m = I * SUB + (SUB // 2 if rev else SUB // 2 - 1)
            rel = lcum[rows(I), :] - jnp.broadcast_to(lcum[m:m + 1, :], (SUB, dkt))
            q_mid.append(q[rows(I), :] * jnp.exp2(rel))
            k_mid.append(k[rows(I), :] * jnp.exp2(-rel))
        k_mid_all = phys_concat(k_mid).astype(BF16)
        kbd_mid_t = jnp.where(rk == ck, jnp.concatenate([k_mid_all] * HEADS, axis=0), 0)
        sc_diag = _dot_nt(phys_concat(q_mid).astype(BF16), kbd_mid_t)

    st = st_ref[...]
    vb = v.astype(BF16)
    rs = _iota_div((REC_W, dkt), 0, REC_DV)
    cs = _iota_div((REC_W, dkt), 1, dk)
    o_inter = _dot_nt(phys_concat(q_hat).astype(BF16), jnp.where(rs == cs, st.astype(BF16), 0))
    upd = _dot_tn(vb, phys_concat(k_hat).astype(BF16))
    yield

    lane = lax.broadcasted_iota(jnp.int32, (SUB, HEADS * CHUNK), 1)
    col_blk = lax.shift_right_logical(lane, jnp.int32(SUB.bit_length() - 1)) & (NSUB - 1)
    col_p = (NSUB - 1 - col_blk) if rev else col_blk
    key_pos = lane & (SUB - 1)
    row_pos = lax.broadcasted_iota(jnp.int32, (SUB, HEADS * CHUNK), 0)
    causal = (row_pos <= key_pos) if rev else (row_pos >= key_pos)
    p_rows = []
    for p, I in enumerate(phys):
        if min_step_log2 is None:
            groups = SUB // pack
            level = [w_all[(I * groups + n) * SUB:(I * groups + n + 1) * SUB, :] for n in range(groups)]
            bit = pack
            while len(level) > 1:
                take_odd = (lane & bit) != 0
                level = [jnp.where(take_odd, level[2 * n + 1], level[2 * n]) for n in range(len(level) // 2)]
                bit *= 2
            diag = level[0]
        else:
            diag = jnp.where(causal, sc_diag[rows(I), :], 0.0)
        acc_p = jnp.where(col_p == p, diag, 0.0)
        for dist in range(1, p + 1):
            n = lhs_at[(p, dist)]
            acc_p = jnp.where(col_p == p - dist, sc[n * SUB:(n + 1) * SUB, :], acc_p)
        p_rows.append(acc_p)
    p_all = phys_concat(p_rows).astype(BF16)
    rv = _iota_div((HEADS * CHUNK, REC_W), 0, CHUNK)
    cv = _iota_div((HEADS * CHUNK, REC_W), 1, REC_DV)
    vbd = jnp.where(rv == cv, jnp.concatenate([vb] * HEADS, axis=0), 0)
    o_intra = _dot(p_all, vbd)
    yield

    store_out(o_intra + o_inter)
    st_ref[...] = st * jnp.exp2(total[0:1, :]) + upd


def _run_interleaved(chains, skew):
    chains = list(chains)
    done = [False] * len(chains)
    started = 0
    while not all(done):
        started = min(started + skew, len(chains))
        for c in range(started):
            if not done[c]:
                try:
                    next(chains[c])
                except StopIteration:
                    done[c] = True


def _store_row(o_ref, i, o):
    o_ref[i] = o.astype(o_ref.dtype)


def _hg_prep(z_ref, i, lb_ref, d):
    col = lambda n: z_ref[i, :, n * REC_W:(n + 1) * REC_W].astype(F32)
    lb = lb_ref[d:d + 1, :]
    q = _silu(col(0))
    v = col(3)
    zz = col(1 + d)
    f = lb + (1.0 - lb) * jax.nn.sigmoid(zz)
    g = jnp.log2(jnp.maximum(f, FORGET_MIN))
    k = (1.0 - lb) * jax.nn.sigmoid(-zz)
    return q, k, g, v


def _log_sigmoid(x):
    return jnp.minimum(x, 0.0) - jnp.log(1.0 + jnp.exp(-jnp.abs(x)))


def _gla_gate(z_ref, i, wg_ref, bg_ref, d):
    logits = _dot(z_ref[i, :, GLA_COLS - LANE:GLA_COLS], wg_ref[d])
    yield
    return logits + bg_ref[d]


def _gla_prep(z_ref, i, logits):
    dkt = HEADS * GLA_DK
    q = z_ref[i, :, 0:dkt].astype(F32) * (GLA_DK ** -0.5)
    k = z_ref[i, :, dkt:2 * dkt].astype(F32)
    v = z_ref[i, :, 2 * dkt:2 * dkt + REC_W].astype(F32)
    return q, k, _log_sigmoid(logits) * (LOG2E / GLA_TAU), v


def _rec_kernel(*refs, mixer, dk, bb):
    if mixer == "hg":
        zf_ref, zb_ref, lb_ref, hsum_ref, of_ref, ob_ref, stf, stb = refs
        tri_ref = None
    else:
        zf_ref, zb_ref, wg_ref, bg_ref, tri_ref, hsum_ref, of_ref, ob_ref, stf, stb = refs

    @pl.when(pl.program_id(1) == 0)
    def _():
        stf[...] = jnp.zeros_like(stf)
        stb[...] = jnp.zeros_like(stb)

    head_sum = hsum_ref[...]
    slots = [(i, d, (zf_ref, zb_ref)[d], (of_ref, ob_ref)[d], (stf, stb)[d]) for i in range(bb) for d in range(2)]

    def run(preps, min_step_log2, skew):
        chains = [_rec_chain(prep, functools.partial(_store_row, o_ref, i), st.at[i], head_sum,
                             None if tri_ref is None else tri_ref[d],
                             rev=(d == 1), dk=dk, min_step_log2=min_step_log2)
                  for prep, (i, d, _, o_ref, st) in zip(preps, slots)]
        _run_interleaved(chains, skew=skew)

    if mixer == "hg":
        inputs = [_hg_prep(z_ref, i, lb_ref, d) for i, d, z_ref, _, _ in slots]
        run([functools.partial(tuple, vals) for vals in inputs], float(np.log2(FORGET_MIN)), skew=1)
        return
    gens = [_gla_gate(z_ref, i, wg_ref, bg_ref, d) for i, d, z_ref, _, _ in slots]
    for gen in gens:
        next(gen)
    logits = []
    for gen in gens:
        try:
            next(gen)
        except StopIteration as done:
            logits.append(done.value)
    bounded = jnp.min(functools.reduce(jnp.minimum, logits)) >= GLA_FAST_MIN_LOGIT
    preps = [functools.partial(_gla_prep, z_ref, i, x) for x, (i, _, z_ref, _, _) in zip(logits, slots)]

    @pl.when(bounded)
    def _():
        run(preps, GLA_FAST_MIN_LOG2, skew=len(slots))

    @pl.when(jnp.logical_not(bounded))
    def _():
        run(preps, None, skew=len(slots))


def _head_sum_weight(dk):
    dkt = HEADS * dk
    pack = max(1, MXU_DEPTH // dkt)
    w = np.zeros((pack * dkt, HEADS * CHUNK), np.float32)
    for u in range(pack):
        for h in range(HEADS):
            for j in range(u, CHUNK, pack):
                w[u * dkt + h * dk:u * dkt + (h + 1) * dk, h * CHUNK + j] = 1.0
    return jnp.asarray(w, BF16)


def _recurrence(z, mixer, params, n_ctx):
    B, T, cols = z.shape
    dk = HG_DK if mixer == "hg" else GLA_DK
    dkt = HEADS * dk
    nc, ncc = T // CHUNK, n_ctx // CHUNK
    bb = REC_BB if B % REC_BB == 0 else 1
    head_sum = _head_sum_weight(dk)
    full = lambda shape: pl.BlockSpec(shape, lambda b, c: (0,) * len(shape))
    fwd = lambda b, c: (b, c, 0)
    bwd = lambda b, c: (b, jnp.where(c < ncc, ncc - 1 - c, nc + ncc - 1 - c), 0)
    in_specs = [pl.BlockSpec((bb, CHUNK, cols), fwd), pl.BlockSpec((bb, CHUNK, cols), bwd)]
    in_specs += [full(p.shape) for p in params]
    in_specs += [full(head_sum.shape)]
    return pl.pallas_call(
        functools.partial(_rec_kernel, mixer=mixer, dk=dk, bb=bb),
        grid=(B // bb, nc),
        in_specs=in_specs,
        out_specs=[pl.BlockSpec((bb, CHUNK, REC_W), fwd), pl.BlockSpec((bb, CHUNK, REC_W), bwd)],
        out_shape=[jax.ShapeDtypeStruct((B, T, REC_W), BF16)] * 2,
        scratch_shapes=[pltpu.VMEM((bb, REC_W, dkt), F32), pltpu.VMEM((bb, REC_W, dkt), F32)],
        compiler_params=_cparams(("arbitrary", "arbitrary")),
        name="recurrence_" + mixer,
    )(z, z, *params, head_sum)


def _attn_stage(q_ref, k_ref, vt_ref, o_ref, s_w, m_w, s_r, m_r):
    n_keys = k_ref.shape[2]
    half = n_keys // 2
    parts = [slice(0, half), slice(half, n_keys)]
    groups = [slice(j, j + ATT_Q_GROUP) for j in range(0, q_ref.shape[2], ATT_Q_GROUP)]
    for qs in groups:
        q = q_ref[0, 0, qs, :]
        m = None
        for ks in parts:
            s = _dot_nt(k_ref[0, 0, ks, :], q)
            s_w[ks, qs] = s
            mx = jnp.max(s, axis=0, keepdims=True)
            m = mx if m is None else jnp.maximum(m, mx)
        m_w[0:1, qs] = m
        m_prev = m_r[0:1, qs]
        acc = None
        for ks in parts:
            part = _dot(vt_ref[0, 0, :, ks], jnp.exp2(s_r[ks, qs] - m_prev).astype(BF16))
            acc = part if acc is None else acc + part
        o_t = acc[0:MLA_DV, :] * (1.0 / acc[MLA_DV:MLA_DV + 1, :])
        o_ref[0, qs, :] = o_t.T.astype(o_ref.dtype)


def _attn_kernel(q_ref, k_ref, vt_ref, o_ref, s_a, m_a, s_b, m_b):
    t = pl.program_id(0)

    @pl.when(t == 0)
    def _():
        s_b[...] = jnp.zeros_like(s_b)
        m_b[...] = jnp.zeros_like(m_b)

    @pl.when(t % 2 == 0)
    def _():
        _attn_stage(q_ref, k_ref, vt_ref, o_ref, s_a, m_a, s_b, m_b)

    @pl.when(t % 2 == 1)
    def _():
        _attn_stage(q_ref, k_ref, vt_ref, o_ref, s_b, m_b, s_a, m_a)


def _attention(q, k, vt, row0, n_rows, n_keys):
    B, H, _, _ = q.shape
    tq = ATT_Q_TILE if (n_rows % ATT_Q_TILE == 0 and row0 % ATT_Q_TILE == 0) else ATT_Q_GROUP
    assert n_keys % (2 * LANE) == 0 and row0 % tq == 0 and n_rows % tq == 0
    nq, q0 = n_rows // tq, row0 // tq
    n_tiles = B * H * nq

    def tile(t):
        return t // (H * nq), (t // nq) % H, t % nq

    def score_tile(t):
        return tile(jnp.minimum(t, n_tiles - 1))

    def finish_tile(t):
        return tile(jnp.maximum(t - 1, 0))

    def q_map(t):
        b, h, i = score_tile(t)
        return b, h, q0 + i, 0

    def k_map(t):
        b, h, _ = score_tile(t)
        return b, h, 0, 0

    def vt_map(t):
        b, h, _ = finish_tile(t)
        return b, h, 0, 0

    def o_map(t):
        b, h, i = finish_tile(t)
        return b, i, h

    return pl.pallas_call(
        _attn_kernel,
        grid=(n_tiles + 1,),
        in_specs=[pl.BlockSpec((1, 1, tq, MLA_QK), q_map),
                  pl.BlockSpec((1, 1, n_keys, MLA_QK), k_map),
                  pl.BlockSpec((1, 1, V_ROWS, n_keys), vt_map)],
        out_specs=pl.BlockSpec((1, tq, MLA_DV), o_map),
        out_shape=jax.ShapeDtypeStruct((B, n_rows, H * MLA_DV), BF16),
        scratch_shapes=[pltpu.VMEM((n_keys, tq), F32), pltpu.VMEM((SUBLANE, tq), F32),
                        pltpu.VMEM((n_keys, tq), F32), pltpu.VMEM((SUBLANE, tq), F32)],
        compiler_params=_cparams(("arbitrary",)),
        name="attention",
    )(q, k, vt)


def _head_rms(o, ones_bf, g):
    ms = _dot((o * o).astype(BF16), ones_bf) * (1.0 / REC_DV)
    return o * lax.rsqrt(ms + NORM_EPS) * g


def _out_kernel(*refs, n_ctx_tiles):
    if n_ctx_tiles:
        x_ref, att_ref, xc_ref, attc_ref = refs[0:4]
        rest = refs[4:]
    else:
        x_ref, att_ref = refs[0:2]
        rest = refs[2:]
    (gate_ref, zhg_ref, zmla_ref, zgla_ref, hgf_ref, hgb_ref, glf_ref, glb_ref,
     ones_ref, hgn_ref, gln_ref, w_ref, lng_ref, lnb_ref) = rest[0:14]
    x, att = x_ref[0], att_ref[0]
    if n_ctx_tiles:
        is_ctx = pl.program_id(1) < n_ctx_tiles
        x = jnp.where(is_ctx, xc_ref[0], x)
        att = jnp.where(is_ctx, attc_ref[0], att)
    f32 = lambda ref: ref[0].astype(F32)
    ones_bf = ones_ref[...]
    y_hg = _head_rms(f32(hgf_ref) + f32(hgb_ref), ones_bf, hgn_ref[...]) * _silu(f32(zhg_ref))
    y_mla = att.astype(F32) * _silu(f32(zmla_ref))
    y_gla = _head_rms(f32(glf_ref) + f32(glb_ref), ones_bf, gln_ref[...]) * _silu(f32(zgla_ref))
    proj = (_dot(y_hg.astype(BF16), w_ref[0:REC_W, :])
            + _dot(y_mla.astype(BF16), w_ref[REC_W:REC_W + MLA_W, :])
            + _dot(y_gla.astype(BF16), w_ref[REC_W + MLA_W:, :]))
    r = DEEPNORM_ALPHA * x + gate_ref[0] * proj
    res = _layer_norm(r) * lng_ref[...] + lnb_ref[...]
    o_ref = rest[14]
    o_ref[0] = res
    if n_ctx_tiles:
        oc_ref = rest[15]

        @pl.when(is_ctx)
        def _():
            oc_ref[0] = res


def _output_block(x, att, xc, att_c, mod, z_hg, mla_gate, z_gla, hg_f, hg_b, gl_f, gl_b,
                  hg_norm_g, gla_norm_g, w_out_bf, ln_g, ln_b):
    B, N, _ = x.shape
    T = z_hg.shape[1]
    tm = ROW_TILE
    with_ctx = xc is not None
    nct = (T - N) // tm
    first = 0 if with_ctx else nct
    lat = lambda i: jnp.maximum(i + first - nct, 0)
    ones = np.zeros((REC_W, REC_W), np.float32)
    for r in range(REC_W):
        ones[r, (r // REC_DV) * REC_DV:(r // REC_DV + 1) * REC_DV] = 1.0
    ones = jnp.asarray(ones, BF16)
    row = lambda w, col=0: pl.BlockSpec((1, tm, w), lambda b, i: (b, i + first, col))
    lat_row = lambda w: pl.BlockSpec((1, tm, w), lambda b, i: (b, lat(i), 0))
    ctx_row = lambda w: pl.BlockSpec((1, tm, w), lambda b, i: (b, jnp.minimum(i, nct - 1), 0))
    full = lambda shape: pl.BlockSpec(shape, lambda b, i: (0,) * len(shape))
    mod_row = (lambda b, i: jnp.where(i < nct, B, b)) if with_ctx else (lambda b, i: b)
    in_specs = [lat_row(D_MODEL), lat_row(MLA_W)]
    args = [x, att]
    if with_ctx:
        in_specs += [ctx_row(D_MODEL), ctx_row(MLA_W)]
        args += [xc, att_c]
    in_specs += [pl.BlockSpec((1, 1, D_MODEL), lambda b, i: (mod_row(b, i), 0, 2)),
                 row(REC_W, HG_COLS // REC_W - 1),
                 row(MLA_W),
                 row(REC_W, 2),
                 row(REC_W), row(REC_W), row(REC_W), row(REC_W),
                 full((REC_W, REC_W)), full((1, REC_W)), full((1, REC_W)),
                 full((D_MODEL, D_MODEL)), full((1, D_MODEL)), full((1, D_MODEL))]
    args += [mod, z_hg, mla_gate, z_gla, hg_f, hg_b, gl_f, gl_b, ones,
             jnp.tile(hg_norm_g, HEADS).reshape(1, REC_W), jnp.tile(gla_norm_g, HEADS).reshape(1, REC_W),
             w_out_bf, ln_g.reshape(1, D_MODEL), ln_b.reshape(1, D_MODEL)]
    out_specs = [lat_row(D_MODEL)]
    out_shape = [jax.ShapeDtypeStruct((B, N, D_MODEL), F32)]
    if with_ctx:
        out_specs.append(ctx_row(D_MODEL))
        out_shape.append(jax.ShapeDtypeStruct(xc.shape, F32))
    outs = pl.pallas_call(
        functools.partial(_out_kernel, n_ctx_tiles=nct if with_ctx else 0),
        grid=(B, T // tm - first),
        in_specs=in_specs,
        out_specs=out_specs,
        out_shape=out_shape,
        compiler_params=_cparams(("arbitrary", "arbitrary")),
        name="output_block",
    )(*args)
    return (outs[0], outs[1]) if with_ctx else (outs[0], None)


def _pack_w_in(w):
    o = np.cumsum([0, 256, 256, 256, 256, 256, 256, 128, 64, 512, 128, 128, 256, 16, 16, 256])
    col = lambda i: w[:, int(o[i]):int(o[i + 1])]
    zeros = lambda n: jnp.zeros((w.shape[0], n), w.dtype)
    hg = [col(0), col(1), col(2), col(3), col(4)]
    mla = [col(5), col(6), col(7), col(12), col(13), zeros(LANE - MLA_ROPE - 2 * GLA_GATE_RANK), col(8)]
    gla = [col(9), col(10), col(11), col(14)]
    return jnp.concatenate(hg + mla + gla, axis=1).astype(BF16)


def _pack_wuq(w):
    w4 = w.reshape(MLA_Q_RANK, HEADS, MLA_QK)
    return jnp.concatenate([w4[:, :, :MLA_NOPE].reshape(MLA_Q_RANK, -1),
                            w4[:, :, MLA_NOPE:].reshape(MLA_Q_RANK, -1)], axis=1).astype(BF16)


def _pack_wukv(w):
    w4 = w.reshape(MLA_KV_RANK, HEADS, MLA_NOPE + MLA_DV)
    wk = w4[:, :, :MLA_NOPE].reshape(MLA_KV_RANK, -1).astype(BF16)
    wvt = w4[:, :, MLA_NOPE:].reshape(MLA_KV_RANK, -1).T.astype(BF16)
    return wk, wvt


def _pack_gla_gate(w_a2, b_a):
    dkt = HEADS * GLA_DK
    wg = jnp.zeros((2, LANE, dkt), F32)
    wg = wg.at[0, GLA_A_LANE:GLA_A_LANE + GLA_GATE_RANK].set(w_a2[0])
    wg = wg.at[1, GLA_A_LANE + GLA_GATE_RANK:GLA_A_LANE + 2 * GLA_GATE_RANK].set(w_a2[1])
    return wg.astype(BF16), b_a.reshape(2, 1, dkt)


def _prefix_matrices():
    i = np.arange(CHUNK)
    same = (i[:, None] // SUB) == (i[None, :] // SUB)
    return jnp.asarray(np.stack([same & (i[None, :] <= i[:, None]), same & (i[None, :] >= i[:, None])]), BF16)


def _rope_tables(n_ctx, n):
    rows = n // GRID_W
    pos_r = jnp.repeat(jnp.arange(rows, dtype=F32), GRID_W)
    pos_c = jnp.tile(jnp.arange(GRID_W, dtype=F32), rows)
    inv = 1.0 / (ROPE_BASE ** (jnp.arange(ROPE_FREQS, dtype=F32) / ROPE_FREQS))
    ang = jnp.stack([pos_r, pos_c], axis=-1)[:, :, None] * inv
    cos = jnp.cos(ang)
    sin = jnp.sin(ang)
    cos64 = jnp.concatenate([cos, cos], axis=-1).reshape(n, MLA_ROPE)
    sin64 = jnp.concatenate([-sin, sin], axis=-1).reshape(n, MLA_ROPE)
    cos_t = jnp.tile(cos64, (1, LANE // MLA_ROPE))
    sin_t = jnp.tile(sin64, (1, LANE // MLA_ROPE))
    return (jnp.concatenate([jnp.ones((n_ctx, LANE), F32), cos_t], axis=0),
            jnp.concatenate([jnp.zeros((n_ctx, LANE), F32), sin_t], axis=0))


def kernel(x, c, ctx, c_ctx, w_mod, b_mod, w_in, w_out, ln_g, ln_b, hg_lb_logits, hg_norm_g,
           mla_q_norm_g, mla_kv_norm_g, mla_w_uq, mla_w_ukv, gla_w_a2, gla_b_a, gla_norm_g):
    B, N, _ = x.shape
    Lc = ctx.shape[1]
    assert N % ROW_TILE == 0 and Lc % ROW_TILE == 0 and N % GRID_W == 0
    cos_t, sin_t = _rope_tables(Lc, N)

    lb_soft = jax.nn.softmax(hg_lb_logits.astype(F32), axis=0)
    hg_lb = jnp.clip(jnp.cumsum(lb_soft, axis=0) - lb_soft[0:1], 0.0, 1.0)

    n_rows = -(-(B + 1) // SUBLANE) * SUBLANE
    c_rows = jnp.zeros((n_rows, D_MODEL), F32).at[0:B].set(c).at[B].set(c_ctx)
    mods = _modulation(c_rows, w_mod, b_mod)

    xc = ctx
    for l in range(DEPTH):
        need_ctx = l < DEPTH - 1
        mod = mods[l].reshape(n_rows, 1, 3 * D_MODEL)
        wuk_p, wvt_p = _pack_wukv(mla_w_ukv[l])
        z_hg, z_gla, mla_gate, q, k, vt = _front(
            x, xc, mod, _pack_w_in(w_in[l]), mla_q_norm_g[l], mla_kv_norm_g[l],
            _pack_wuq(mla_w_uq[l]), wuk_p, wvt_p, cos_t, sin_t)

        hg_f, hg_b = _recurrence(z_hg, "hg", (hg_lb[l],), Lc)
        gl_f, gl_b = _recurrence(z_gla, "gla", _pack_gla_gate(gla_w_a2[l], gla_b_a[l]) + (_prefix_matrices(),), Lc)
        att = _attention(q, k, vt, 0, N, Lc + N)
        att_c = _attention(q, k, vt, N, Lc, Lc) if need_ctx else None

        x, xc_new = _output_block(x, att, xc if need_ctx else None, att_c, mod, z_hg, mla_gate, z_gla,
                                  hg_f, hg_b, gl_f, gl_b, hg_norm_g[l], gla_norm_g[l],
                                  w_out[l].astype(BF16), ln_g[l], ln_b[l])
        if need_ctx:
            xc = xc_new
    return x
```
